```python
import math
import jax, jax.numpy as jnp
from jax import lax
import numpy as np

D_MODEL = 4096
BATCH = 4
SEQ = 2048
DEPTH = 2
DEC_BATCH = 32
DEC_SEQ = 32
PAST_LEN = 4096

CHUNK = 64
Q_BLOCK = 128
N_A_LAYERS = DEPTH // 2
N_B_LAYERS = DEPTH - N_A_LAYERS
D_FF = ((8 * D_MODEL // 3 + 127) // 128) * 128
GLA_HEADS = 4
GLA_DK = D_MODEL // 2
GLA_DV = D_MODEL
GLA_DK_HEAD = GLA_DK // GLA_HEADS
GLA_DV_HEAD = GLA_DV // GLA_HEADS
GLA_GATE_RANK = 16
GLA_GATE_TEMP = 16.0
GLA_IN = 2 * GLA_DK + 2 * GLA_DV + GLA_GATE_RANK
DIFF_HEAD_DIM = 128
DIFF_HEADS = D_MODEL // (2 * DIFF_HEAD_DIM)
D_QK = DIFF_HEADS * 2 * DIFF_HEAD_DIM
D_V = DIFF_HEADS * 2 * DIFF_HEAD_DIM
ROT_DIM = DIFF_HEAD_DIM // 4
ROPE_THETA = 500000.0
N_MOD = 9
EPS = 1e-6
NEG = -1e30

kernel_name = "yoco_gla_diffattn_macaron_adaln_stream_step"


def rmsnorm(x, g):
    xf = x.astype(jnp.float32)
    y = xf * lax.rsqrt(jnp.mean(xf * xf, axis=-1, keepdims=True) + EPS)
    return (y * g.astype(jnp.float32)).astype(x.dtype)


def modulate(x, shift, scale):
    return x * (1 + scale[:, None, :]) + shift[:, None, :]


def swiglu_ffn(h, g, shift, scale, w_up, w_down):
    hn = modulate(rmsnorm(h, g), shift, scale)
    a, b = jnp.split(hn @ w_up, 2, axis=-1)
    return (jax.nn.silu(a) * b) @ w_down


def partial_rope(x, pos):
    inv = ROPE_THETA ** (-jnp.arange(0, ROT_DIM, 2, dtype=jnp.float32) / ROT_DIM)
    ang = pos.astype(jnp.float32)[:, None] * inv[None, :]
    cos = jnp.cos(ang)[None, :, None, None, :]
    sin = jnp.sin(ang)[None, :, None, None, :]
    xr = x[..., :ROT_DIM].astype(jnp.float32)
    x1, x2 = xr[..., :ROT_DIM // 2], xr[..., ROT_DIM // 2:]
    rot = jnp.concatenate([x1 * cos - x2 * sin, x2 * cos + x1 * sin], axis=-1).astype(x.dtype)
    return jnp.concatenate([rot, x[..., ROT_DIM:]], axis=-1)


def gla_recurrence(q, k, v, log_a, S0):
    B, T, H, dk = q.shape
    dv = v.shape[-1]
    nC = -(-T // CHUNK)
    pad = nC * CHUNK - T
    if pad:
        pw = ((0, 0), (0, pad), (0, 0), (0, 0))
        q, k, v, log_a = (jnp.pad(t, pw) for t in (q, k, v, log_a))
    q = q.reshape(B, nC, CHUNK, H, dk)
    k = k.reshape(B, nC, CHUNK, H, dk)
    v = v.reshape(B, nC, CHUNK, H, dv)
    b = jnp.cumsum(log_a.reshape(B, nC, CHUNK, H, dk), axis=2)
    b_last = b[:, :, -1]
    qe = q * jnp.exp(b)
    ke = k * jnp.exp(-b)
    kd = k * jnp.exp(b_last[:, :, None] - b)
    causal = jnp.tril(jnp.ones((CHUNK, CHUNK), dtype=bool))
    A = jnp.einsum('bnthd,bnshd->bnhts', qe, ke)
    A = jnp.where(causal, A, 0.0)
    o_intra = jnp.einsum('bnhts,bnshv->bnthv', A, v)

    def step(S, xs):
        qe_n, kd_n, v_n, bl_n = xs
        o = jnp.einsum('bthd,bhdv->bthv', qe_n, S)
        S = jnp.exp(bl_n)[..., None] * S + jnp.einsum('bthd,bthv->bhdv', kd_n, v_n)
        return S, o

    xs = (jnp.moveaxis(qe, 1, 0), jnp.moveaxis(kd, 1, 0), jnp.moveaxis(v, 1, 0), jnp.moveaxis(b_last, 1, 0))
    S, o_inter = lax.scan(step, S0, xs)
    o = o_intra + jnp.moveaxis(o_inter, 0, 1)
    return o.reshape(B, nC * CHUNK, H, dv)[:, :T], S


def gla_mixer(hn, S0, w_in, w_gate, b_gate, g_head, w_out):
    B, T, _ = hn.shape
    p = hn @ w_in
    q, k, v, r, glr = jnp.split(p, [GLA_DK, 2 * GLA_DK, 2 * GLA_DK + GLA_DV, 2 * GLA_DK + 2 * GLA_DV], axis=-1)
    log_a = jax.nn.log_sigmoid((glr @ w_gate + b_gate).astype(jnp.float32)) / GLA_GATE_TEMP
    f32 = jnp.float32
    qh = q.astype(f32).reshape(B, T, GLA_HEADS, GLA_DK_HEAD) * (GLA_DK_HEAD ** -0.5)
    kh = k.astype(f32).reshape(B, T, GLA_HEADS, GLA_DK_HEAD)
    vh = v.astype(f32).reshape(B, T, GLA_HEADS, GLA_DV_HEAD)
    la = log_a.reshape(B, T, GLA_HEADS, GLA_DK_HEAD)
    o, S = gla_recurrence(qh, kh, vh, la, S0.astype(f32))
    o = rmsnorm(o.astype(hn.dtype), g_head).reshape(B, T, GLA_DV) * jax.nn.silu(r)
    return o @ w_out, S.astype(S0.dtype)


def shared_kv(h, pos, g_kv, w_kv):
    B, T, _ = h.shape
    kv = rmsnorm(h, g_kv) @ w_kv
    k = kv[..., :D_QK].reshape(B, T, DIFF_HEADS, 2, DIFF_HEAD_DIM)
    v = kv[..., D_QK:].reshape(B, T, DIFF_HEADS, 2 * DIFF_HEAD_DIM)
    return partial_rope(k, pos), v


def diff_attn_prompt(q, k, v, lam):
    B, T, H, _, d = q.shape
    nb = T // Q_BLOCK
    scale = d ** -0.5
    qb = jnp.moveaxis(q.reshape(B, nb, Q_BLOCK, H, 2, d), 1, 0)
    k_chunk = jnp.arange(T) // CHUNK

    def block(args):
        qi, i = args
        q_chunk = (i * Q_BLOCK + jnp.arange(Q_BLOCK)) // CHUNK
        mask = k_chunk[None, :] <= q_chunk[:, None]
        s = jnp.einsum('bqhcd,bkhcd->bhcqk', qi, k, preferred_element_type=jnp.float32) * scale
        p = jax.nn.softmax(jnp.where(mask, s, NEG), axis=-1)
        w = p[:, :, 0] - lam * p[:, :, 1]
        return jnp.einsum('bhqk,bkhv->bqhv', w.astype(v.dtype), v)

    o = lax.map(block, (qb, jnp.arange(nb)))
    return jnp.moveaxis(o, 0, 1).reshape(B, T, H, 2 * d)


def diff_attn_sample(q, k_past, v_past, k_new, v_new, lam):
    d = q.shape[-1]
    scale = d ** -0.5
    s_p = jnp.einsum('bqhcd,bkhcd->bhcqk', q, k_past, preferred_element_type=jnp.float32) * scale
    s_n = jnp.einsum('bqhcd,bkhcd->bhcqk', q, k_new, preferred_element_type=jnp.float32) * scale
    m = jnp.maximum(s_p.max(-1), s_n.max(-1))[..., None]
    e_p = jnp.exp(s_p - m)
    e_n = jnp.exp(s_n - m)
    z = e_p.sum(-1, keepdims=True) + e_n.sum(-1, keepdims=True)
    p_p = e_p / z
    p_n = e_n / z
    w_p = p_p[:, :, 0] - lam * p_p[:, :, 1]
    w_n = p_n[:, :, 0] - lam * p_n[:, :, 1]
    return (jnp.einsum('bhqk,bkhv->bqhv', w_p.astype(v_past.dtype), v_past)
            + jnp.einsum('bhqk,bkhv->bqhv', w_n.astype(v_new.dtype), v_new))


def diff_mixer(hn, pos, k_sh, v_sh, k_past, v_past, w_dq, lq1, lk1, lq2, lk2, g_head, w_do, lam_init):
    B, T, _ = hn.shape
    q = partial_rope((hn @ w_dq).reshape(B, T, DIFF_HEADS, 2, DIFF_HEAD_DIM), pos)
    f32 = jnp.float32
    lam = (jnp.exp(jnp.sum(lq1.astype(f32) * lk1.astype(f32)))
           - jnp.exp(jnp.sum(lq2.astype(f32) * lk2.astype(f32))) + lam_init)
    if k_past is None:
        o = diff_attn_prompt(q, k_sh, v_sh, lam)
    else:
        o = diff_attn_sample(q, k_past, v_past, k_sh, v_sh, lam)
    o = rmsnorm(o, g_head) * (1.0 - lam_init)
    return o.reshape(B, T, D_V) @ w_do


def trunk(x, c, pos, gla_state0, k_past, v_past,
          w_mod, b_mod, g_norm, w_ffn_up, w_ffn_down,
          w_gla_in, w_gla_gate, b_gla_gate, g_gla_head, w_gla_out,
          g_kv, w_kv, w_dq, lambda_q1, lambda_k1, lambda_q2, lambda_k2, g_diff_head, w_do, g_final):
    B = x.shape[0]
    h = x
    gla_states = []
    k_sh = v_sh = None
    for l in range(DEPTH):
        mod = (jax.nn.silu(c) @ w_mod[l] + b_mod[l]).reshape(B, N_MOD, D_MODEL)
        sh1, sc1, ga1, sh2, sc2, ga2, sh3, sc3, ga3 = (mod[:, i] for i in range(N_MOD))
        h = h + 0.5 * ga1[:, None] * swiglu_ffn(h, g_norm[l, 0], sh1, sc1, w_ffn_up[l, 0], w_ffn_down[l, 0])
        hn = modulate(rmsnorm(h, g_norm[l, 1]), sh2, sc2)
        if l < N_A_LAYERS:
            o, S = gla_mixer(hn, gla_state0[l], w_gla_in[l], w_gla_gate[l], b_gla_gate[l],
                             g_gla_head[l], w_gla_out[l])
            gla_states.append(S)
        else:
            j = l - N_A_LAYERS
            lam_init = 0.8 - 0.6 * math.exp(-0.3 * l)
            o = diff_mixer(hn, pos, k_sh, v_sh, k_past, v_past, w_dq[j], lambda_q1[j], lambda_k1[j],
                           lambda_q2[j], lambda_k2[j], g_diff_head[j], w_do[j], lam_init)
        h = h + ga2[:, None] * o
        h = h + 0.5 * ga3[:, None] * swiglu_ffn(h, g_norm[l, 2], sh3, sc3, w_ffn_up[l, 1], w_ffn_down[l, 1])
        if l == N_A_LAYERS - 1:
            k_sh, v_sh = shared_kv(h, pos, g_kv, w_kv)
    y = rmsnorm(h, g_final)
    return y, jnp.stack(gla_states), k_sh, v_sh


def setup_inputs(seed: int = 0) -> dict:
    key = jax.random.key(seed)
    ks = jax.random.split(key, 32)
    f32 = jnp.float32

    def nrm(k, shape, scale):
        return jax.random.normal(k, shape, f32) * scale

    return {
        "x_prompt": nrm(ks[0], (BATCH, SEQ, D_MODEL), 1.0),
        "x_sample": nrm(ks[1], (DEC_BATCH, DEC_SEQ, D_MODEL), 1.0),
        "c_prompt": nrm(ks[2], (BATCH, D_MODEL), 1.0),
        "c_sample": nrm(ks[3], (DEC_BATCH, D_MODEL), 1.0),
        "state_gla": nrm(ks[4], (N_A_LAYERS, DEC_BATCH, GLA_HEADS, GLA_DK_HEAD, GLA_DV_HEAD), 1.0),
        "cache_k": nrm(ks[5], (DEC_BATCH, PAST_LEN, DIFF_HEADS, 2, DIFF_HEAD_DIM), 1.0),
        "cache_v": nrm(ks[6], (DEC_BATCH, PAST_LEN, DIFF_HEADS, 2 * DIFF_HEAD_DIM), 1.0),
        "w_mod": nrm(ks[7], (DEPTH, D_MODEL, N_MOD * D_MODEL), D_MODEL ** -0.5),
        "b_mod": nrm(ks[8], (DEPTH, N_MOD * D_MODEL), 0.01),
        "g_norm": 1.0 + nrm(ks[9], (DEPTH, 3, D_MODEL), 0.01),
        "w_ffn_up": nrm(ks[10], (DEPTH, 2, D_MODEL, 2 * D_FF), D_MODEL ** -0.5),
        "w_ffn_down": nrm(ks[11], (DEPTH, 2, D_FF, D_MODEL), D_FF ** -0.5),
        "w_gla_in": nrm(ks[12], (N_A_LAYERS, D_MODEL, GLA_IN), D_MODEL ** -0.5),
        "w_gla_gate": nrm(ks[13], (N_A_LAYERS, GLA_GATE_RANK, GLA_DK), GLA_GATE_RANK ** -0.5),
        "b_gla_gate": nrm(ks[14], (N_A_LAYERS, GLA_DK), 0.01),
        "g_gla_head": 1.0 + nrm(ks[15], (N_A_LAYERS, GLA_DV_HEAD), 0.01),
        "w_gla_out": nrm(ks[16], (N_A_LAYERS, GLA_DV, D_MODEL), GLA_DV ** -0.5),
        "g_kv": 1.0 + nrm(ks[17], (D_MODEL,), 0.01),
        "w_kv": nrm(ks[18], (D_MODEL, D_QK + D_V), D_MODEL ** -0.5),
        "w_dq": nrm(ks[19], (N_B_LAYERS, D_MODEL, D_QK), D_MODEL ** -0.5),
        "lambda_q1": nrm(ks[20], (N_B_LAYERS, DIFF_HEAD_DIM), 0.1),
        "lambda_k1": nrm(ks[21], (N_B_LAYERS, DIFF_HEAD_DIM), 0.1),
        "lambda_q2": nrm(ks[22], (N_B_LAYERS, DIFF_HEAD_DIM), 0.1),
        "lambda_k2": nrm(ks[23], (N_B_LAYERS, DIFF_HEAD_DIM), 0.1),
        "g_diff_head": 1.0 + nrm(ks[24], (N_B_LAYERS, 2 * DIFF_HEAD_DIM), 0.01),
        "w_do": nrm(ks[25], (N_B_LAYERS, D_V, D_MODEL), D_V ** -0.5),
        "g_final": 1.0 + nrm(ks[26], (D_MODEL,), 0.01),
    }


def reference(x_prompt, x_sample, c_prompt, c_sample, state_gla, cache_k, cache_v,
              w_mod, b_mod, g_norm, w_ffn_up, w_ffn_down,
              w_gla_in, w_gla_gate, b_gla_gate, g_gla_head, w_gla_out,
              g_kv, w_kv, w_dq, lambda_q1, lambda_k1, lambda_q2, lambda_k2, g_diff_head, w_do, g_final):
    weights = (w_mod, b_mod, g_norm, w_ffn_up, w_ffn_down,
               w_gla_in, w_gla_gate, b_gla_gate, g_gla_head, w_gla_out,
               g_kv, w_kv, w_dq, lambda_q1, lambda_k1, lambda_q2, lambda_k2, g_diff_head, w_do, g_final)
    B, T, _ = x_prompt.shape
    s0 = jnp.zeros((N_A_LAYERS, B, GLA_HEADS, GLA_DK_HEAD, GLA_DV_HEAD), x_prompt.dtype)
    pos_p = jnp.arange(T)
    y_prompt, state_gla_prompt, k_prompt, v_prompt = trunk(
        x_prompt, c_prompt, pos_p, s0, None, None, *weights)
    past = cache_k.shape[1]
    pos_s = past + jnp.arange(x_sample.shape[1])
    y_sample, state_gla_sample, k_sample, v_sample = trunk(
        x_sample, c_sample, pos_s, state_gla, cache_k, cache_v, *weights)
    return (y_prompt, y_sample, state_gla_prompt, state_gla_sample, k_prompt, v_prompt, k_sample, v_sample)
```

```python
import functools
import math

import jax
import jax.numpy as jnp
from jax import lax
from jax.experimental import pallas as pl
from jax.experimental.pallas import tpu as pltpu

F32 = jnp.float32
BF16 = jnp.bfloat16

CHUNK = 64
GLA_HEADS = 4
GLA_GATE_TEMP = 16.0
DIFF_HEAD_DIM = 128
ROT_DIM = DIFF_HEAD_DIM // 4
ROPE_THETA = 500000.0
N_MOD = 9
EPS = 1e-6
NEG = -1e30

LANE = 128
V7X_VMEM_BYTES = 64 * 1024 * 1024
VMEM_LIMIT = 56 * 1024 * 1024


def _params(*sem):
    return pltpu.CompilerParams(dimension_semantics=sem, vmem_limit_bytes=VMEM_LIMIT)


def _tile(n, target, align):
    if n <= target:
        return n
    t = (target // align) * align
    while t >= align:
        if n % t == 0:
            return t
        t -= align
    return n


def _row_tile(B, T, tm):
    if T >= tm:
        tt = _tile(T, tm, 16)
        return 1, tt
    bb = _tile(B, max(tm // T, 1), 1)
    return bb, T


def _dot(a, b):
    return jnp.dot(a, b, preferred_element_type=F32)


def _dot_nt(a, b):
    return lax.dot_general(a, b, (((1,), (1,)), ((), ())), preferred_element_type=F32)


def _dot_tn(a, b):
    return lax.dot_general(a, b, (((0,), (0,)), ((), ())), preferred_element_type=F32)


def _silu(x):
    return x * jax.nn.sigmoid(x)


def _mod_body(c_ref, w_ref, b_ref, o_ref):
    x = _silu(c_ref[...]).astype(BF16)
    o_ref[...] = _dot(x, w_ref[...].astype(BF16)) + b_ref[...]


def _mod_call(c, w_mod, b_mod):
    L, D, N = w_mod.shape
    R = c.shape[0]
    tn = _tile(N, 512, LANE)
    return pl.pallas_call(
        _mod_body,
        grid=(L, N // tn),
        in_specs=[
            pl.BlockSpec((R, D), lambda l, j: (0, 0)),
            pl.BlockSpec((None, D, tn), lambda l, j: (l, 0, j)),
            pl.BlockSpec((None, 1, tn), lambda l, j: (l, 0, j)),
        ],
        out_specs=pl.BlockSpec((None, R, tn), lambda l, j: (l, 0, j)),
        out_shape=jax.ShapeDtypeStruct((L, R, N), F32),
        compiler_params=_params("parallel", "parallel"),
        name="mod",
    )(c, w_mod, b_mod.reshape(L, 1, N))


def _norm_body(h_ref, g_ref, *rest, modulated):
    o_ref = rest[-1]
    x = h_ref[...]
    y = x * lax.rsqrt(jnp.mean(x * x, axis=-1, keepdims=True) + EPS) * g_ref[...]
    if modulated:
        sh_ref, sc_ref = rest[0], rest[1]
        y = y * (1 + sc_ref[...]) + sh_ref[...]
    o_ref[...] = y.astype(o_ref.dtype)


def _norm_call(h, g, mod=None, i_shift=None, i_scale=None, out_dtype=BF16):
    B, T, D = h.shape
    bb, tt = _row_tile(B, T, 256)
    nt = T // tt
    row = lambda i: (i // nt, i % nt, 0)
    in_specs = [pl.BlockSpec((bb, tt, D), row), pl.BlockSpec((1, 1, D), lambda i: (0, 0, 0))]
    args = [h, g.reshape(1, 1, D)]
    if mod is not None:
        for idx in (i_shift, i_scale):
            in_specs.append(pl.BlockSpec((bb, None, 1, D), lambda i, idx=idx: (i // nt, idx, 0, 0)))
            args.append(mod)
    return pl.pallas_call(
        functools.partial(_norm_body, modulated=mod is not None),
        grid=((B // bb) * nt,),
        in_specs=in_specs,
        out_specs=pl.BlockSpec((bb, tt, D), row),
        out_shape=jax.ShapeDtypeStruct((B, T, D), out_dtype),
        compiler_params=_params("parallel"),
        name="norm",
    )(*args)


def _x2d(x_ref):
    bb, tt, K = x_ref.shape
    return x_ref[...].reshape(bb * tt, K)


def _mm_plain_body(x_ref, w_ref, o_ref):
    y = _dot(_x2d(x_ref), w_ref[...].astype(BF16))
    o_ref[...] = y.reshape(o_ref.shape).astype(o_ref.dtype)


def _mm_swiglu_body(x_ref, wa_ref, wb_ref, o_ref):
    x = _x2d(x_ref)
    a = _dot(x, wa_ref[...].astype(BF16))
    b = _dot(x, wb_ref[...].astype(BF16))
    o_ref[...] = (_silu(a) * b).reshape(o_ref.shape).astype(o_ref.dtype)


def _mm_resid_body(x_ref, w_ref, h_ref, g_ref, o_ref, *, scale):
    y = _dot(_x2d(x_ref), w_ref[...].astype(BF16))
    o_ref[...] = h_ref[...] + (scale * g_ref[...]) * y.reshape(o_ref.shape)


def _mm_rope_body(x_ref, w_ref, c_ref, sa_ref, sb_ref, o_ref):
    y = _dot(_x2d(x_ref), w_ref[...].astype(BF16))
    tn = y.shape[-1]
    reps = tn // DIFF_HEAD_DIM
    half = ROT_DIM // 2
    up = pltpu.roll(y, tn - half, 1).reshape(o_ref.shape)
    dn = pltpu.roll(y, half, 1).reshape(o_ref.shape)
    y = y.reshape(o_ref.shape)
    ct = jnp.tile(c_ref[...], (1, reps))[None]
    sa = jnp.tile(sa_ref[...], (1, reps))[None]
    sb = jnp.tile(sb_ref[...], (1, reps))[None]
    o_ref[...] = y * ct + up * sa + dn * sb


def _wspec(w, prefix, tn, col_block0):
    K = w.shape[-2]
    block = (None,) * len(prefix) + (K, tn)
    return pl.BlockSpec(block, lambda i, j: tuple(prefix) + (0, j + col_block0))


def _mm_call(body, x, weights, n_cols, tn, out_dtype, extra_specs=(), extra_args=(), tm=1024, name="mm"):
    B, T, K = x.shape
    bb, tt = _row_tile(B, T, tm)
    nt = T // tt
    assert n_cols % tn == 0
    row = lambda i, j: (i // nt, i % nt, 0)
    in_specs = [pl.BlockSpec((bb, tt, K), row)]
    args = [x]
    for w, prefix, col0 in weights:
        assert col0 % tn == 0 and w.shape[-2] == K
        in_specs.append(_wspec(w, prefix, tn, col0 // tn))
        args.append(w)
    for mk in extra_specs:
        in_specs.append(mk(bb, tt, nt, tn))
    args.extend(extra_args)
    return pl.pallas_call(
        body,
        grid=((B // bb) * nt, n_cols // tn),
        in_specs=in_specs,
        out_specs=pl.BlockSpec((bb, tt, tn), lambda i, j: (i // nt, i % nt, j)),
        out_shape=jax.ShapeDtypeStruct((B, T, n_cols), out_dtype),
        compiler_params=_params("parallel", "arbitrary"),
        name=name,
    )(*args)


def _h_spec(bb, tt, nt, tn):
    return pl.BlockSpec((bb, tt, tn), lambda i, j: (i // nt, i % nt, j))


def _gate_spec(idx):
    def mk(bb, tt, nt, tn):
        return pl.BlockSpec((bb, None, 1, tn), lambda i, j: (i // nt, idx, 0, j))
    return mk


def _table_spec(bb, tt, nt, tn):
    return pl.BlockSpec((tt, DIFF_HEAD_DIM), lambda i, j: (i % nt, 0))


def _mm_plain(x, w, prefix, col0, n_cols, name):
    tn = _tile(n_cols, 512, LANE)
    return _mm_call(_mm_plain_body, x, [(w, prefix, col0)], n_cols, tn, F32, name=name)


def _mm_swiglu(x, w_up, prefix):
    dff = w_up.shape[-1] // 2
    tn = _tile(dff, 256, LANE)
    return _mm_call(_mm_swiglu_body, x, [(w_up, prefix, 0), (w_up, prefix, dff)], dff, tn, BF16,
                    name="ffn_up_swiglu")


def _mm_resid(x, w, prefix, h, mod, gate_idx, scale, name):
    n = w.shape[-1]
    tn = _tile(n, 512, LANE)
    return _mm_call(functools.partial(_mm_resid_body, scale=scale), x, [(w, prefix, 0)], n, tn, F32,
                    extra_specs=(_h_spec, _gate_spec(gate_idx)), extra_args=(h, mod), name=name)


def _mm_rope(x, w, prefix, n_cols, tables, name):
    tn = _tile(n_cols, 512, LANE)
    return _mm_call(_mm_rope_body, x, [(w, prefix, 0)], n_cols, tn, F32,
                    extra_specs=(_table_spec,) * 3, extra_args=tables, name=name)


def _mm_down_body(x_ref, w_ref, h_ref, g_ref, o_ref, wb_ref, *, scale, n_chunks):
    @pl.when(pl.program_id(1) == 0)
    def _():
        rows = w_ref.shape[0] // n_chunks
        for c in range(n_chunks):
            wb_ref[c * rows:(c + 1) * rows, :] = w_ref[c * rows:(c + 1) * rows, :].astype(BF16)

    y = _dot(_x2d(x_ref), wb_ref[...])
    o_ref[...] = h_ref[...] + (scale * g_ref[...]) * y.reshape(o_ref.shape)


def _mm_down(x, w, prefix, h, mod, gate_idx, scale):
    B, T, K = x.shape
    N = w.shape[-1]
    tn = _tile(N, 512, LANE)
    bb, tt = _row_tile(B, T, 256)
    nt = T // tt
    n_chunks = 8 if K % (8 * 16) == 0 else 1
    wblock = (None,) * len(prefix) + (K, tn)
    return pl.pallas_call(
        functools.partial(_mm_down_body, scale=scale, n_chunks=n_chunks),
        grid=(N // tn, (B // bb) * nt),
        in_specs=[
            pl.BlockSpec((bb, tt, K), lambda j, i: (i // nt, i % nt, 0)),
            pl.BlockSpec(wblock, lambda j, i: tuple(prefix) + (0, j), pipeline_mode=pl.Buffered(1)),
            pl.BlockSpec((bb, tt, tn), lambda j, i: (i // nt, i % nt, j)),
            pl.BlockSpec((bb, None, 1, tn), lambda j, i: (i // nt, gate_idx, 0, j)),
        ],
        out_specs=pl.BlockSpec((bb, tt, tn), lambda j, i: (i // nt, i % nt, j)),
        out_shape=jax.ShapeDtypeStruct((B, T, N), F32),
        scratch_shapes=[pltpu.VMEM((K, tn), BF16)],
        compiler_params=_params("arbitrary", "arbitrary"),
        name="ffn_down",
    )(x, w, h, mod)


def _gla_gate_body(x_ref, w16_ref, wg_ref, bg_ref, o_ref):
    glr = _dot(_x2d(x_ref), w16_ref[...].astype(BF16))
    z = _dot(glr.astype(BF16), wg_ref[...].astype(BF16)) + bg_ref[...]
    o_ref[...] = (jax.nn.log_sigmoid(z) / GLA_GATE_TEMP).reshape(o_ref.shape)


def _gla_gate_call(x, w16, w_gate, b_gate):
    B, T, K = x.shape
    R, N = w_gate.shape
    bb, tt = _row_tile(B, T, 512)
    nt = T // tt
    return pl.pallas_call(
        _gla_gate_body,
        grid=((B // bb) * nt,),
        in_specs=[
            pl.BlockSpec((bb, tt, K), lambda i: (i // nt, i % nt, 0)),
            pl.BlockSpec((K, R), lambda i: (0, 0)),
            pl.BlockSpec((R, N), lambda i: (0, 0)),
            pl.BlockSpec((1, N), lambda i: (0, 0)),
        ],
        out_specs=pl.BlockSpec((bb, tt, N), lambda i: (i // nt, i % nt, 0)),
        out_shape=jax.ShapeDtypeStruct((B, T, N), F32),
        compiler_params=_params("parallel"),
        name="gla_gate",
    )(x, w16, w_gate, b_gate.reshape(1, N))


def _gla_body(q_ref, k_ref, v_ref, r_ref, la_ref, gh_ref, *rest, has_s0):
    if has_s0:
        s0_ref, o_ref, s_ref = rest
    else:
        o_ref, s_ref = rest
    C, dk = q_ref.shape

    @pl.when(pl.program_id(2) == 0)
    def _():
        if has_s0:
            s_ref[...] = s0_ref[...]
        else:
            s_ref[...] = jnp.zeros_like(s_ref)

    la = la_ref[...]
    la_hi = la.astype(BF16)
    rem = la - la_hi.astype(F32)
    la_mid = rem.astype(BF16)
    la_lo = (rem - la_mid.astype(F32)).astype(BF16)
    row = lax.broadcasted_iota(jnp.int32, (C, C), 0)
    col = lax.broadcasted_iota(jnp.int32, (C, C), 1)
    causal = row >= col
    tri = causal.astype(BF16)
    b = _dot(tri, la_hi) + _dot(tri, la_mid) + _dot(tri, la_lo)
    b_last = b[C - 1:C, :]

    q = q_ref[...] * (dk ** -0.5)
    k = k_ref[...]
    qe = (q * jnp.exp(b)).astype(BF16)
    ke = (k * jnp.exp(-b)).astype(BF16)
    kd = (k * jnp.exp(b_last - b)).astype(BF16)
    vb = v_ref[...].astype(BF16)

    a = jnp.where(causal, _dot_nt(qe, ke), 0.0)
    s = s_ref[...]
    o = _dot(a.astype(BF16), vb) + _dot(qe, s.astype(BF16))
    decay_col = jnp.transpose(jnp.exp(jnp.broadcast_to(b_last, (LANE, dk))))[:, :1]
    s_ref[...] = decay_col * s + _dot_tn(kd, vb)

    y = o * lax.rsqrt(jnp.mean(o * o, axis=-1, keepdims=True) + EPS) * gh_ref[...]
    o_ref[...] = (y * _silu(r_ref[...])).astype(o_ref.dtype)


def _gla_call(p, la, g_head, s0, dk_all, dv_all):
    B, T, _ = p.shape
    H = GLA_HEADS
    dk, dv = dk_all // H, dv_all // H
    C = min(CHUNK, T)
    assert T % C == 0
    nC = T // C
    kq, kk = 0, dk_all // dk
    kv, kr = (2 * dk_all) // dv, (2 * dk_all + dv_all) // dv
    in_specs = [
        pl.BlockSpec((None, C, dk), lambda b, h, n: (b, n, kq + h)),
        pl.BlockSpec((None, C, dk), lambda b, h, n: (b, n, kk + h)),
        pl.BlockSpec((None, C, dv), lambda b, h, n: (b, n, kv + h)),
        pl.BlockSpec((None, C, dv), lambda b, h, n: (b, n, kr + h)),
        pl.BlockSpec((None, C, dk), lambda b, h, n: (b, n, h)),
        pl.BlockSpec((1, dv), lambda b, h, n: (0, 0)),
    ]
    args = [p, p, p, p, la, g_head.reshape(1, dv)]
    if s0 is not None:
        in_specs.append(pl.BlockSpec((None, None, dk, dv), lambda b, h, n: (b, h, 0, 0)))
        args.append(s0)
    return pl.pallas_call(
        functools.partial(_gla_body, has_s0=s0 is not None),
        grid=(B, H, nC),
        in_specs=in_specs,
        out_specs=[
            pl.BlockSpec((None, C, dv), lambda b, h, n: (b, n, h)),
            pl.BlockSpec((None, None, dk, dv), lambda b, h, n: (b, h, 0, 0)),
        ],
        out_shape=[
            jax.ShapeDtypeStruct((B, T, dv_all), BF16),
            jax.ShapeDtypeStruct((B, H, dk, dv), F32),
        ],
        compiler_params=_params("parallel", "parallel", "arbitrary"),
        name="gla",
    )(*args)


def _lambda(lq1, lk1, lq2, lk2, lam_init):
    s1 = jnp.sum(lq1[...] * lk1[...], axis=-1, keepdims=True)
    s2 = jnp.sum(lq2[...] * lk2[...], axis=-1, keepdims=True)
    return jnp.exp(s1) - jnp.exp(s2) + lam_init


def _online_softmax_step(s, vb, m_ref, l_ref, acc_ref, idx):
    m_prev = m_ref[idx]
    m_new = jnp.maximum(m_prev, jnp.max(s, axis=-1, keepdims=True))
    alpha = jnp.exp(m_prev - m_new)
    p = jnp.exp(s - m_new)
    l_ref[idx] = alpha * l_ref[idx] + jnp.sum(p, axis=-1, keepdims=True)
    acc_ref[idx] = alpha * acc_ref[idx] + _dot(p.astype(BF16), vb)
    m_ref[idx] = m_new


def _head_norm(o, g, lam_init):
    return o * lax.rsqrt(jnp.mean(o * o, axis=-1, keepdims=True) + EPS) * g * (1.0 - lam_init)


def _attn_prompt_body(q_ref, k_ref, v_ref, lq1, lk1, lq2, lk2, gh_ref, o_ref,
                      kb_ref, vb_ref, m_ref, l_ref, acc_ref, *, lam_init):
    tq = q_ref.shape[0]
    d = DIFF_HEAD_DIM
    qi = pl.program_id(2)

    @pl.when(qi == 0)
    def _():
        kb_ref[...] = k_ref[...].astype(BF16)
        vb_ref[...] = v_ref[...].astype(BF16)

    q = q_ref[...] * (d ** -0.5)
    qs = (q[:, :d].astype(BF16), q[:, d:].astype(BF16))
    m_ref[...] = jnp.full_like(m_ref, NEG)
    l_ref[...] = jnp.zeros_like(l_ref)
    acc_ref[...] = jnp.zeros_like(acc_ref)

    row = lax.broadcasted_iota(jnp.int32, (tq, tq), 0) // CHUNK
    col = lax.broadcasted_iota(jnp.int32, (tq, tq), 1) // CHUNK
    visible = col <= row

    def step(j, masked):
        start = pl.multiple_of(j * tq, tq)
        kb = kb_ref[pl.ds(start, tq), :]
        vb = vb_ref[pl.ds(start, tq), :]
        for c in range(2):
            s = _dot_nt(qs[c], kb[:, c * d:(c + 1) * d])
            if masked:
                s = jnp.where(visible, s, NEG)
            _online_softmax_step(s, vb, m_ref, l_ref, acc_ref, c)

    def loop_body(j, carry):
        step(j, False)
        return carry

    lax.fori_loop(0, qi, loop_body, 0)
    step(qi, True)

    lam = _lambda(lq1, lk1, lq2, lk2, lam_init)
    o = acc_ref[0] / l_ref[0] - lam * (acc_ref[1] / l_ref[1])
    o_ref[...] = _head_norm(o, gh_ref[...], lam_init).astype(o_ref.dtype)


def _lam_specs(nd):
    zero = (0,) * 2
    return [pl.BlockSpec((1, DIFF_HEAD_DIM), lambda *a: zero) for _ in range(4)]


def _attn_prompt_call(q, k, v, lams, g_head, lam_init):
    B, T, DQ = q.shape
    hd = 2 * DIFF_HEAD_DIM
    H = DQ // hd
    tq = _tile(T, 256, CHUNK)
    nq = T // tq
    return pl.pallas_call(
        functools.partial(_attn_prompt_body, lam_init=lam_init),
        grid=(B, H, nq),
        in_specs=[
            pl.BlockSpec((None, tq, hd), lambda b, h, i: (b, i, h)),
            pl.BlockSpec((None, T, hd), lambda b, h, i: (b, 0, h)),
            pl.BlockSpec((None, T, hd), lambda b, h, i: (b, 0, h)),
            *_lam_specs(3),
            pl.BlockSpec((1, hd), lambda b, h, i: (0, 0)),
        ],
        out_specs=pl.BlockSpec((None, tq, hd), lambda b, h, i: (b, i, h)),
        out_shape=jax.ShapeDtypeStruct((B, T, DQ), BF16),
        scratch_shapes=[
            pltpu.VMEM((T, hd), BF16),
            pltpu.VMEM((T, hd), BF16),
            pltpu.VMEM((2, tq, 1), F32),
            pltpu.VMEM((2, tq, 1), F32),
            pltpu.VMEM((2, tq, hd), F32),
        ],
        compiler_params=_params("parallel", "parallel", "arbitrary"),
        name="diff_attn_prompt",
    )(q, k, v, *lams, g_head.reshape(1, hd))


def _attn_sample_body(q_ref, kp_ref, vp_ref, kn_ref, vn_ref, lq1, lk1, lq2, lk2, gh_ref, o_ref,
                      qb_ref, m_ref, l_ref, acc_ref, *, lam_init, n_heads):
    d = DIFF_HEAD_DIM
    hd = 2 * d
    tq = q_ref.shape[0]
    j = pl.program_id(1)

    @pl.when(j == 0)
    def _():
        qb_ref[...] = (q_ref[...] * (d ** -0.5)).astype(BF16)
        m_ref[...] = jnp.full_like(m_ref, NEG)
        l_ref[...] = jnp.zeros_like(l_ref)
        acc_ref[...] = jnp.zeros_like(acc_ref)

    def process(k_ref, v_ref):
        for h in range(n_heads):
            kb = k_ref[:, h * hd:(h + 1) * hd].astype(BF16)
            vb = v_ref[:, h * hd:(h + 1) * hd].astype(BF16)
            s1 = _dot_nt(qb_ref[:, h * hd:h * hd + d], kb[:, :d])
            s2 = _dot_nt(qb_ref[:, h * hd + d:(h + 1) * hd], kb[:, d:])
            s = jnp.concatenate([s1, s2], axis=0)
            _online_softmax_step(s, vb, m_ref, l_ref, acc_ref, h)

    process(kp_ref, vp_ref)

    @pl.when(j == pl.num_programs(1) - 1)
    def _():
        process(kn_ref, vn_ref)
        lam = _lambda(lq1, lk1, lq2, lk2, lam_init)
        g = gh_ref[...]
        for h in range(n_heads):
            on = acc_ref[h] / l_ref[h]
            o = on[:tq] - lam * on[tq:]
            o_ref[:, h * hd:(h + 1) * hd] = _head_norm(o, g, lam_init).astype(o_ref.dtype)


def _attn_sample_call(q, k_past, v_past, k_new, v_new, lams, g_head, lam_init):
    B, T, DQ = q.shape
    P = k_past.shape[1]
    hd = 2 * DIFF_HEAD_DIM
    H = DQ // hd
    tkv = _tile(P, 256, 16)
    return pl.pallas_call(
        functools.partial(_attn_sample_body, lam_init=lam_init, n_heads=H),
        grid=(B, P // tkv),
        in_specs=[
            pl.BlockSpec((None, T, DQ), lambda b, j: (b, 0, 0)),
            pl.BlockSpec((None, tkv, DQ), lambda b, j: (b, j, 0)),
            pl.BlockSpec((None, tkv, DQ), lambda b, j: (b, j, 0)),
            pl.BlockSpec((None, T, DQ), lambda b, j: (b, 0, 0)),
            pl.BlockSpec((None, T, DQ), lambda b, j: (b, 0, 0)),
            *_lam_specs(2),
            pl.BlockSpec((1, hd), lambda b, j: (0, 0)),
        ],
        out_specs=pl.BlockSpec((None, T, DQ), lambda b, j: (b, 0, 0)),
        out_shape=jax.ShapeDtypeStruct((B, T, DQ), BF16),
        scratch_shapes=[
            pltpu.VMEM((T, DQ), BF16),
            pltpu.VMEM((H, 2 * T, 1), F32),
            pltpu.VMEM((H, 2 * T, 1), F32),
            pltpu.VMEM((H, 2 * T, hd), F32),
        ],
        compiler_params=_params("parallel", "arbitrary"),
        name="diff_attn_sample",
    )(q, k_past, v_past, k_new, v_new, *lams, g_head.reshape(1, hd))


def _rope_tables(pos):
    half = ROT_DIM // 2
    inv = ROPE_THETA ** (-jnp.arange(0, ROT_DIM, 2, dtype=F32) / ROT_DIM)
    ang = pos.astype(F32)[:, None] * inv[None, :]
    cos, sin = jnp.cos(ang), jnp.sin(ang)
    T = pos.shape[0]
    ones = jnp.ones((T, DIFF_HEAD_DIM - ROT_DIM), F32)
    zeros_h = jnp.zeros((T, half), F32)
    zeros_r = jnp.zeros((T, DIFF_HEAD_DIM - ROT_DIM), F32)
    c = jnp.concatenate([cos, cos, ones], axis=1)
    sa = jnp.concatenate([-sin, zeros_h, zeros_r], axis=1)
    sb = jnp.concatenate([zeros_h, sin, zeros_r], axis=1)
    return c, sa, sb


def _trunk(x, mods, pos, s0, k_past, v_past, W):
    (g_norm, w_ffn_up, w_ffn_down, w_gla_in, w_gla_gate, b_gla_gate, g_gla_head, w_gla_out,
     g_kv, w_kv, w_dq, lams, g_diff_head, w_do, g_final) = W
    B, T, D = x.shape
    depth = g_norm.shape[0]
    n_a = w_gla_in.shape[0]
    tables = _rope_tables(pos)
    dk_all = w_gla_gate.shape[-1]
    dv_all = w_gla_out.shape[-2]
    d_qk = w_dq.shape[-1]

    h = x
    states = []
    k_sh = v_sh = None
    for l in range(depth):
        mod = mods[l]

        def ffn(h, s, i0):
            hn = _norm_call(h, g_norm[l, i0 // 3], mod, i0, i0 + 1)
            hid = _mm_swiglu(hn, w_ffn_up, (l, s))
            return _mm_down(hid, w_ffn_down, (l, s), h, mod, i0 + 2, 0.5)

        h = ffn(h, 0, 0)
        hn = _norm_call(h, g_norm[l, 1], mod, 3, 4)
        if l < n_a:
            p = _mm_plain(hn, w_gla_in, (l,), 0, 2 * dk_all + 2 * dv_all, "gla_in")
            w16 = w_gla_in[l][:, 2 * dk_all + 2 * dv_all:]
            la = _gla_gate_call(hn, w16, w_gla_gate[l], b_gla_gate[l])
            og, S = _gla_call(p, la, g_gla_head[l], None if s0 is None else s0[l], dk_all, dv_all)
            states.append(S)
            h = _mm_resid(og, w_gla_out, (l,), h, mod, 5, 1.0, "gla_out")
        else:
            jj = l - n_a
            lam_init = 0.8 - 0.6 * math.exp(-0.3 * l)
            q = _mm_rope(hn, w_dq, (jj,), d_qk, tables, "diff_q")
            lam_l = tuple(t[jj].reshape(1, DIFF_HEAD_DIM) for t in lams)
            if k_past is None:
                oa = _attn_prompt_call(q, k_sh, v_sh, lam_l, g_diff_head[jj], lam_init)
            else:
                oa = _attn_sample_call(q, k_past, v_past, k_sh, v_sh, lam_l, g_diff_head[jj], lam_init)
            h = _mm_resid(oa, w_do, (jj,), h, mod, 5, 1.0, "diff_out")
        h = ffn(h, 1, 6)
        if l == n_a - 1:
            xn = _norm_call(h, g_kv)
            k_sh = _mm_rope(xn, w_kv, (), d_qk, tables, "shared_k")
            v_sh = _mm_plain(xn, w_kv, (), d_qk, w_kv.shape[-1] - d_qk, "shared_v")
    y = _norm_call(h, g_final, out_dtype=F32)
    return y, jnp.stack(states), k_sh, v_sh


def kernel(x_prompt, x_sample, c_prompt, c_sample, state_gla, cache_k, cache_v, w_mod, b_mod, g_norm, w_ffn_up, w_ffn_down, w_gla_in, w_gla_gate, b_gla_gate, g_gla_head, w_gla_out, g_kv, w_kv, w_dq, lambda_q1, lambda_k1, lambda_q2, lambda_k2, g_diff_head, w_do, g_final):
    B, T, D = x_prompt.shape
    Bs, Ts, _ = x_sample.shape
    P = cache_k.shape[1]
    H = cache_k.shape[2]
    hd = 2 * DIFF_HEAD_DIM

    rows = B + Bs
    rows_pad = -(-rows // 16) * 16
    c_all = jnp.concatenate([c_prompt, c_sample, jnp.zeros((rows_pad - rows, D), F32)], axis=0)
    mod_all = _mod_call(c_all, w_mod, b_mod)
    L = w_mod.shape[0]
    mods_p = [mod_all[l, :B].reshape(B, N_MOD, 1, D) for l in range(L)]
    mods_s = [mod_all[l, B:rows].reshape(Bs, N_MOD, 1, D) for l in range(L)]

    W = (g_norm, w_ffn_up, w_ffn_down, w_gla_in, w_gla_gate, b_gla_gate, g_gla_head, w_gla_out,
         g_kv, w_kv, w_dq, (lambda_q1, lambda_k1, lambda_q2, lambda_k2), g_diff_head, w_do, g_final)

    y_p, s_p, k_p, v_p = _trunk(x_prompt, mods_p, jnp.arange(T), None, None, None, W)
    y_s, s_s, k_s, v_s = _trunk(x_sample, mods_s, P + jnp.arange(Ts), state_gla,
                                cache_k.reshape(Bs, P, H * hd), cache_v.reshape(Bs, P, H * hd), W)
    return (y_p, y_s, s_p, s_s,
            k_p.reshape(B, T, H, 2, DIFF_HEAD_DIM), v_p.reshape(B, T, H, hd),
            k_s.reshape(Bs, Ts, H, 2, DIFF_HEAD_DIM), v_s.reshape(Bs, Ts, H, hd))
```

```python
import functools
import math

import jax
import jax.numpy as jnp
from jax import lax
from jax.experimental import pallas as pl
from jax.experimental.pallas import tpu as pltpu

F32 = jnp.float32
BF16 = jnp.bfloat16

CHUNK = 64
GLA_HEADS = 4
GLA_GATE_TEMP = 16.0
DIFF_HEAD_DIM = 128
ROT_DIM = DIFF_HEAD_DIM // 4
ROPE_THETA = 500000.0
N_MOD = 9
EPS = 1e-6
NEG = -1e30

LANE = 128
V7X_VMEM_BYTES = 64 * 1024 * 1024
VMEM_LIMIT = 56 * 1024 * 1024


def _params(*sem):
    return pltpu.CompilerParams(dimension_semantics=sem, vmem_limit_bytes=VMEM_LIMIT)


def _tile(n, target, align):
    if n <= target:
        return n
    t = (target // align) * align
    while t >= align:
        if n % t == 0:
            return t
        t -= align
    return n


def _row_tile(B, T, tm):
    if T >= tm:
        tt = _tile(T, tm, 16)
        return 1, tt
    bb = _tile(B, max(tm // T, 1), 1)
    return bb, T


def _dot(a, b):
    return jnp.dot(a, b, preferred_element_type=F32)


def _dot_nt(a, b):
    return lax.dot_general(a, b, (((1,), (1,)), ((), ())), preferred_element_type=F32)


def _dot_tn(a, b):
    return lax.dot_general(a, b, (((0,), (0,)), ((), ())), preferred_element_type=F32)


def _silu(x):
    return x * jax.nn.sigmoid(x)


def _mod_body(c_ref, w_ref, b_ref, o_ref):
    x = _silu(c_ref[...]).astype(BF16)
    o_ref[...] = _dot(x, w_ref[...].astype(BF16)) + b_ref[...]


def _mod_call(c, w_mod, b_mod):
    L, D, N = w_mod.shape
    R = c.shape[0]
    tn = _tile(N, 512, LANE)
    return pl.pallas_call(
        _mod_body,
        grid=(L, N // tn),
        in_specs=[
            pl.BlockSpec((R, D), lambda l, j: (0, 0)),
            pl.BlockSpec((None, D, tn), lambda l, j: (l, 0, j)),
            pl.BlockSpec((None, 1, tn), lambda l, j: (l, 0, j)),
        ],
        out_specs=pl.BlockSpec((None, R, tn), lambda l, j: (l, 0, j)),
        out_shape=jax.ShapeDtypeStruct((L, R, N), F32),
        compiler_params=_params("parallel", "parallel"),
        name="mod",
    )(c, w_mod, b_mod.reshape(L, 1, N))


def _norm_body(h_ref, g_ref, *rest, modulated):
    o_ref = rest[-1]
    x = h_ref[...]
    y = x * lax.rsqrt(jnp.mean(x * x, axis=-1, keepdims=True) + EPS) * g_ref[...]
    if modulated:
        sh_ref, sc_ref = rest[0], rest[1]
        y = y * (1 + sc_ref[...]) + sh_ref[...]
    o_ref[...] = y.astype(o_ref.dtype)


def _norm_call(h, g, mod=None, i_shift=None, i_scale=None, out_dtype=BF16):
    B, T, D = h.shape
    bb, tt = _row_tile(B, T, 256)
    nt = T // tt
    row = lambda i: (i // nt, i % nt, 0)
    in_specs = [pl.BlockSpec((bb, tt, D), row), pl.BlockSpec((1, 1, D), lambda i: (0, 0, 0))]
    args = [h, g.reshape(1, 1, D)]
    if mod is not None:
        for idx in (i_shift, i_scale):
            in_specs.append(pl.BlockSpec((bb, None, 1, D), lambda i, idx=idx: (i // nt, idx, 0, 0)))
            args.append(mod)
    return pl.pallas_call(
        functools.partial(_norm_body, modulated=mod is not None),
        grid=((B // bb) * nt,),
        in_specs=in_specs,
        out_specs=pl.BlockSpec((bb, tt, D), row),
        out_shape=jax.ShapeDtypeStruct((B, T, D), out_dtype),
        compiler_params=_params("parallel"),
        name="norm",
    )(*args)


def _x2d(x_ref):
    bb, tt, K = x_ref.shape
    return x_ref[...].reshape(bb * tt, K)


def _mm_plain_body(x_ref, w_ref, o_ref):
    y = _dot(_x2d(x_ref), w_ref[...].astype(BF16))
    o_ref[...] = y.reshape(o_ref.shape).astype(o_ref.dtype)


def _mm_swiglu_body(x_ref, wa_ref, wb_ref, o_ref):
    x = _x2d(x_ref)
    a = _dot(x, wa_ref[...].astype(BF16))
    b = _dot(x, wb_ref[...].astype(BF16))
    o_ref[...] = (_silu(a) * b).reshape(o_ref.shape).astype(o_ref.dtype)


def _mm_resid_body(x_ref, w_ref, h_ref, g_ref, o_ref, *, scale):
    y = _dot(_x2d(x_ref), w_ref[...].astype(BF16))
    o_ref[...] = h_ref[...] + (scale * g_ref[...]) * y.reshape(o_ref.shape)


ROPE_ROW_CHUNKS = 4


def _mm_rope_body(x_ref, w_ref, c_ref, sa_ref, sb_ref, o_ref):
    bb, tt, tn = o_ref.shape
    d = DIFF_HEAD_DIM
    half = ROT_DIM // 2
    x = _x2d(x_ref)
    wb = w_ref[...].astype(BF16)
    rows = (bb * tt) // ROPE_ROW_CHUNKS
    for c in range(ROPE_ROW_CHUNKS):
        y = _dot(x[c * rows:(c + 1) * rows], wb)
        if bb == 1:
            dst = (slice(0, 1), slice(c * rows, (c + 1) * rows))
            tab = slice(c * rows, (c + 1) * rows)
            shape = (1, rows, d)
        else:
            nb = rows // tt
            dst = (slice(c * nb, (c + 1) * nb), slice(None))
            tab = slice(None)
            shape = (nb, tt, d)
        ct, sa, sb = c_ref[tab, :][None], sa_ref[tab, :][None], sb_ref[tab, :][None]
        for g in range(tn // d):
            yg = y[:, g * d:(g + 1) * d]
            up = pltpu.roll(yg, d - half, 1).reshape(shape)
            dn = pltpu.roll(yg, half, 1).reshape(shape)
            o_ref[dst + (slice(g * d, (g + 1) * d),)] = yg.reshape(shape) * ct + up * sa + dn * sb


def _wspec(w, prefix, tn, col_block0):
    K = w.shape[-2]
    block = (None,) * len(prefix) + (K, tn)
    return pl.BlockSpec(block, lambda i, j: tuple(prefix) + (0, j + col_block0))


def _mm_call(body, x, weights, n_cols, tn, out_dtype, extra_specs=(), extra_args=(), tm=1024, name="mm"):
    B, T, K = x.shape
    bb, tt = _row_tile(B, T, tm)
    nt = T // tt
    assert n_cols % tn == 0
    row = lambda i, j: (i // nt, i % nt, 0)
    in_specs = [pl.BlockSpec((bb, tt, K), row)]
    args = [x]
    for w, prefix, col0 in weights:
        assert col0 % tn == 0 and w.shape[-2] == K
        in_specs.append(_wspec(w, prefix, tn, col0 // tn))
        args.append(w)
    for mk in extra_specs:
        in_specs.append(mk(bb, tt, nt, tn))
    args.extend(extra_args)
    return pl.pallas_call(
        body,
        grid=((B // bb) * nt, n_cols // tn),
        in_specs=in_specs,
        out_specs=pl.BlockSpec((bb, tt, tn), lambda i, j: (i // nt, i % nt, j)),
        out_shape=jax.ShapeDtypeStruct((B, T, n_cols), out_dtype),
        compiler_params=_params("parallel", "arbitrary"),
        name=name,
    )(*args)


def _h_spec(bb, tt, nt, tn):
    return pl.BlockSpec((bb, tt, tn), lambda i, j: (i // nt, i % nt, j))


def _gate_spec(idx):
    def mk(bb, tt, nt, tn):
        return pl.BlockSpec((bb, None, 1, tn), lambda i, j: (i // nt, idx, 0, j))
    return mk


def _table_spec(bb, tt, nt, tn):
    return pl.BlockSpec((tt, DIFF_HEAD_DIM), lambda i, j: (i % nt, 0))


def _mm_plain(x, w, prefix, col0, n_cols, name):
    tn = _tile(n_cols, 512, LANE)
    return _mm_call(_mm_plain_body, x, [(w, prefix, col0)], n_cols, tn, F32, name=name)


def _mm_swiglu(x, w_up, prefix):
    dff = w_up.shape[-1] // 2
    tn = _tile(dff, 256, LANE)
    return _mm_call(_mm_swiglu_body, x, [(w_up, prefix, 0), (w_up, prefix, dff)], dff, tn, BF16,
                    name="ffn_up_swiglu")


def _mm_resid(x, w, prefix, h, mod, gate_idx, scale, name):
    n = w.shape[-1]
    tn = _tile(n, 512, LANE)
    return _mm_call(functools.partial(_mm_resid_body, scale=scale), x, [(w, prefix, 0)], n, tn, F32,
                    extra_specs=(_h_spec, _gate_spec(gate_idx)), extra_args=(h, mod), name=name)


def _mm_rope(x, w, prefix, n_cols, tables, name):
    tn = _tile(n_cols, 512, LANE)
    return _mm_call(_mm_rope_body, x, [(w, prefix, 0)], n_cols, tn, F32,
                    extra_specs=(_table_spec,) * 3, extra_args=tables, name=name)


def _mm_down_body(x_ref, w_ref, h_ref, g_ref, o_ref, wb_ref, *, scale, n_chunks):
    @pl.when(pl.program_id(1) == 0)
    def _():
        rows = w_ref.shape[0] // n_chunks
        for c in range(n_chunks):
            wb_ref[c * rows:(c + 1) * rows, :] = w_ref[c * rows:(c + 1) * rows, :].astype(BF16)

    y = _dot(_x2d(x_ref), wb_ref[...])
    o_ref[...] = h_ref[...] + (scale * g_ref[...]) * y.reshape(o_ref.shape)


def _mm_down(x, w, prefix, h, mod, gate_idx, scale):
    B, T, K = x.shape
    N = w.shape[-1]
    wide = B * T > 2048
    tn = _tile(N, 512 if wide else 256, LANE)
    w_mode = dict(pipeline_mode=pl.Buffered(1)) if wide else {}
    bb, tt = _row_tile(B, T, 256)
    nt = T // tt
    n_chunks = 8 if K % (8 * 16) == 0 else 1
    wblock = (None,) * len(prefix) + (K, tn)
    return pl.pallas_call(
        functools.partial(_mm_down_body, scale=scale, n_chunks=n_chunks),
        grid=(N // tn, (B // bb) * nt),
        in_specs=[
            pl.BlockSpec((bb, tt, K), lambda j, i: (i // nt, i % nt, 0)),
            pl.BlockSpec(wblock, lambda j, i: tuple(prefix) + (0, j), **w_mode),
            pl.BlockSpec((bb, tt, tn), lambda j, i: (i // nt, i % nt, j)),
            pl.BlockSpec((bb, None, 1, tn), lambda j, i: (i // nt, gate_idx, 0, j)),
        ],
        out_specs=pl.BlockSpec((bb, tt, tn), lambda j, i: (i // nt, i % nt, j)),
        out_shape=jax.ShapeDtypeStruct((B, T, N), F32),
        scratch_shapes=[pltpu.VMEM((K, tn), BF16)],
        compiler_params=_params("arbitrary", "arbitrary"),
        name="ffn_down",
    )(x, w, h, mod)


def _gla_gate_body(x_ref, w16_ref, wg_ref, bg_ref, o_ref):
    glr = _dot(_x2d(x_ref), w16_ref[...].astype(BF16))
    z = _dot(glr.astype(BF16), wg_ref[...].astype(BF16)) + bg_ref[...]
    o_ref[...] = (jax.nn.log_sigmoid(z) / GLA_GATE_TEMP).reshape(o_ref.shape)


def _gla_gate_call(x, w16, w_gate, b_gate):
    B, T, K = x.shape
    R, N = w_gate.shape
    bb, tt = _row_tile(B, T, 512)
    nt = T // tt
    return pl.pallas_call(
        _gla_gate_body,
        grid=((B // bb) * nt,),
        in_specs=[
            pl.BlockSpec((bb, tt, K), lambda i: (i // nt, i % nt, 0)),
            pl.BlockSpec((K, R), lambda i: (0, 0)),
            pl.BlockSpec((R, N), lambda i: (0, 0)),
            pl.BlockSpec((1, N), lambda i: (0, 0)),
        ],
        out_specs=pl.BlockSpec((bb, tt, N), lambda i: (i // nt, i % nt, 0)),
        out_shape=jax.ShapeDtypeStruct((B, T, N), F32),
        compiler_params=_params("parallel"),
        name="gla_gate",
    )(x, w16, w_gate, b_gate.reshape(1, N))


def _gla_body(q_ref, k_ref, v_ref, r_ref, la_ref, gh_ref, *rest, has_s0):
    if has_s0:
        s0_ref, o_ref, s_ref = rest
    else:
        o_ref, s_ref = rest
    C, dk = q_ref.shape

    @pl.when(pl.program_id(2) == 0)
    def _():
        if has_s0:
            s_ref[...] = s0_ref[...]
        else:
            s_ref[...] = jnp.zeros_like(s_ref)

    la = la_ref[...]
    la_hi = la.astype(BF16)
    rem = la - la_hi.astype(F32)
    la_mid = rem.astype(BF16)
    la_lo = (rem - la_mid.astype(F32)).astype(BF16)
    row = lax.broadcasted_iota(jnp.int32, (C, C), 0)
    col = lax.broadcasted_iota(jnp.int32, (C, C), 1)
    causal = row >= col
    tri = causal.astype(BF16)
    b = _dot(tri, la_hi) + _dot(tri, la_mid) + _dot(tri, la_lo)
    b_last = b[C - 1:C, :]

    q = q_ref[...] * (dk ** -0.5)
    k = k_ref[...]
    qe = (q * jnp.exp(b)).astype(BF16)
    ke = (k * jnp.exp(-b)).astype(BF16)
    kd = (k * jnp.exp(b_last - b)).astype(BF16)
    vb = v_ref[...].astype(BF16)

    a = jnp.where(causal, _dot_nt(qe, ke), 0.0)
    s = s_ref[...]
    o = _dot(a.astype(BF16), vb) + _dot(qe, s.astype(BF16))
    decay_col = jnp.transpose(jnp.exp(jnp.broadcast_to(b_last, (LANE, dk))))[:, :1]
    s_ref[...] = decay_col * s + _dot_tn(kd, vb)

    y = o * lax.rsqrt(jnp.mean(o * o, axis=-1, keepdims=True) + EPS) * gh_ref[...]
    o_ref[...] = (y * _silu(r_ref[...])).astype(o_ref.dtype)


def _gla_call(p, la, g_head, s0_all, layer, dk_all, dv_all):
    B, T, _ = p.shape
    H = GLA_HEADS
    dk, dv = dk_all // H, dv_all // H
    C = min(CHUNK, T)
    assert T % C == 0
    nC = T // C
    kq, kk = 0, dk_all // dk
    kv, kr = (2 * dk_all) // dv, (2 * dk_all + dv_all) // dv
    in_specs = [
        pl.BlockSpec((None, C, dk), lambda b, h, n: (b, n, kq + h)),
        pl.BlockSpec((None, C, dk), lambda b, h, n: (b, n, kk + h)),
        pl.BlockSpec((None, C, dv), lambda b, h, n: (b, n, kv + h)),
        pl.BlockSpec((None, C, dv), lambda b, h, n: (b, n, kr + h)),
        pl.BlockSpec((None, C, dk), lambda b, h, n: (b, n, h)),
        pl.BlockSpec((1, dv), lambda b, h, n: (0, 0)),
    ]
    args = [p, p, p, p, la, g_head.reshape(1, dv)]
    if s0_all is not None:
        in_specs.append(pl.BlockSpec((None, None, None, dk, dv), lambda b, h, n: (layer, b, h, 0, 0)))
        args.append(s0_all)
    return pl.pallas_call(
        functools.partial(_gla_body, has_s0=s0_all is not None),
        grid=(B, H, nC),
        in_specs=in_specs,
        out_specs=[
            pl.BlockSpec((None, C, dv), lambda b, h, n: (b, n, h)),
            pl.BlockSpec((None, None, dk, dv), lambda b, h, n: (b, h, 0, 0)),
        ],
        out_shape=[
            jax.ShapeDtypeStruct((B, T, dv_all), BF16),
            jax.ShapeDtypeStruct((B, H, dk, dv), F32),
        ],
        compiler_params=_params("parallel", "parallel", "arbitrary"),
        name="gla",
    )(*args)


def _lambda(lq1, lk1, lq2, lk2, lam_init):
    s1 = jnp.sum(lq1[...] * lk1[...], axis=-1, keepdims=True)
    s2 = jnp.sum(lq2[...] * lk2[...], axis=-1, keepdims=True)
    return jnp.exp(s1) - jnp.exp(s2) + lam_init


def _head_norm(o, g, lam_init):
    return o * lax.rsqrt(jnp.mean(o * o, axis=-1, keepdims=True) + EPS) * g * (1.0 - lam_init)


def _attn_prompt_body(q_ref, k_ref, v_ref, lq1, lk1, lq2, lk2, gh_ref, o_ref,
                      kb_ref, vb_ref, *, lam_init, nq):
    tq = q_ref.shape[0]
    d = DIFF_HEAD_DIM
    qi = pl.program_id(2)

    @pl.when(qi == 0)
    def _():
        kb_ref[...] = k_ref[...].astype(BF16)
        vb_ref[...] = v_ref[...].astype(BF16)

    q = q_ref[...] * (d ** -0.5)
    q1 = q[:, :d].astype(BF16)
    q2 = q[:, d:].astype(BF16)
    lam = _lambda(lq1, lk1, lq2, lk2, lam_init)
    row = (lax.broadcasted_iota(jnp.int32, (2 * tq, tq), 0) % tq) // CHUNK
    col = lax.broadcasted_iota(jnp.int32, (2 * tq, tq), 1) // CHUNK
    visible = col <= row

    for i in range(nq):
        @pl.when(qi == i)
        def _(i=i):
            n_keys = (i + 1) * tq
            kb = kb_ref[:n_keys, :]
            s = jnp.concatenate([_dot_nt(q1, kb[:, :d]), _dot_nt(q2, kb[:, d:])], axis=0)
            s_diag = jnp.where(visible, s[:, n_keys - tq:], NEG)
            s = s_diag if i == 0 else jnp.concatenate([s[:, :n_keys - tq], s_diag], axis=1)
            e = jnp.exp(s - jnp.max(s, axis=-1, keepdims=True))
            inv = 1.0 / jnp.sum(e, axis=-1, keepdims=True)
            w = e[:tq] * inv[:tq] - e[tq:] * (lam * inv[tq:])
            o = _dot(w.astype(BF16), vb_ref[:n_keys, :])
            o_ref[...] = _head_norm(o, gh_ref[...], lam_init).astype(o_ref.dtype)


def _lam_specs(nd):
    zero = (0,) * 2
    return [pl.BlockSpec((1, DIFF_HEAD_DIM), lambda *a: zero) for _ in range(4)]


def _attn_prompt_call(q, k, v, lams, g_head, lam_init):
    B, T, DQ = q.shape
    hd = 2 * DIFF_HEAD_DIM
    H = DQ // hd
    tq = _tile(T, 512, CHUNK)
    nq = T // tq
    return pl.pallas_call(
        functools.partial(_attn_prompt_body, lam_init=lam_init, nq=nq),
        grid=(B, H, nq),
        in_specs=[
            pl.BlockSpec((None, tq, hd), lambda b, h, i: (b, i, h)),
            pl.BlockSpec((None, T, hd), lambda b, h, i: (b, 0, h)),
            pl.BlockSpec((None, T, hd), lambda b, h, i: (b, 0, h)),
            *_lam_specs(3),
            pl.BlockSpec((1, hd), lambda b, h, i: (0, 0)),
        ],
        out_specs=pl.BlockSpec((None, tq, hd), lambda b, h, i: (b, i, h)),
        out_shape=jax.ShapeDtypeStruct((B, T, DQ), BF16),
        scratch_shapes=[pltpu.VMEM((T, hd), BF16), pltpu.VMEM((T, hd), BF16)],
        compiler_params=_params("parallel", "parallel", "arbitrary"),
        name="diff_attn_prompt",
    )(q, k, v, *lams, g_head.reshape(1, hd))


SUBLANES = 8
HEAD_GROUP = SUBLANES


def _rows_at_sublane(ref, r):
    n, s, lanes = ref.shape
    return ref.reshape(n * s, lanes)[pl.ds(r, n, stride=s), :]


def _attn_sample_body(q_ref, ka_ref, kb_ref, vlo_ref, vhi_ref, kn_ref, vn_ref, lq1, lk1, lq2, lk2, gh_ref,
                      o_ref, qb_ref, m_ref, l_ref, acc_ref, *, lam_init):
    d = DIFF_HEAD_DIM
    hd = 2 * d
    tq = q_ref.shape[0]
    tkv = ka_ref.shape[0]
    j = pl.program_id(2)

    @pl.when(j == 0)
    def _():
        qb_ref[...] = (q_ref[...] * (d ** -0.5)).astype(BF16)
        m_ref[...] = jnp.full_like(m_ref, NEG)
        l_ref[...] = jnp.zeros_like(l_ref)
        acc_ref[...] = jnp.zeros_like(acc_ref)

    def cached(h):
        k_tile = (ka_ref, kb_ref)[(2 * h) // SUBLANES]
        r = (2 * h) % SUBLANES
        k1 = _rows_at_sublane(k_tile, r)
        k2 = _rows_at_sublane(k_tile, r + 1)
        v = jnp.concatenate([_rows_at_sublane(vlo_ref, h), _rows_at_sublane(vhi_ref, h)], axis=1)
        return k1, k2, v

    def fresh(h):
        return kn_ref[:, h * hd:h * hd + d], kn_ref[:, h * hd + d:(h + 1) * hd], vn_ref[:, h * hd:(h + 1) * hd]

    def process(get):
        scores, values = [], []
        for h in range(HEAD_GROUP):
            k1, k2, v = get(h)
            scores.append(_dot_nt(qb_ref[:, h * hd:h * hd + d], k1.astype(BF16)))
            scores.append(_dot_nt(qb_ref[:, h * hd + d:(h + 1) * hd], k2.astype(BF16)))
            values.append(v.astype(BF16))
        s = jnp.concatenate(scores, axis=0)
        m_prev = m_ref[...]
        m_new = jnp.maximum(m_prev, jnp.max(s, axis=-1, keepdims=True))
        alpha = jnp.exp(m_prev - m_new)
        p = jnp.exp(s - m_new)
        l_ref[...] = alpha * l_ref[...] + jnp.sum(p, axis=-1, keepdims=True)
        pb = p.astype(BF16)
        pv = jnp.concatenate([_dot(pb[h * 2 * tq:(h + 1) * 2 * tq], values[h]) for h in range(HEAD_GROUP)], axis=0)
        acc_ref[...] = alpha * acc_ref[...] + pv
        m_ref[...] = m_new

    process(cached)

    @pl.when(j == pl.num_programs(2) - 1)
    def _():
        process(fresh)
        lam = _lambda(lq1, lk1, lq2, lk2, lam_init)
        on = acc_ref[...] / l_ref[...]
        for h in range(HEAD_GROUP):
            o = on[h * 2 * tq:h * 2 * tq + tq] - lam * on[h * 2 * tq + tq:(h + 1) * 2 * tq]
            o_ref[:, h * hd:(h + 1) * hd] = _head_norm(o, gh_ref[...], lam_init).astype(o_ref.dtype)


def _attn_sample_call(q, k_past, v_past, k_new, v_new, lams, g_head, lam_init):
    B, T, DQ = q.shape
    P, H = k_past.shape[1], k_past.shape[2]
    d = DIFF_HEAD_DIM
    hd = 2 * d
    assert H % HEAD_GROUP == 0
    G = H // HEAD_GROUP
    gw = HEAD_GROUP * hd
    tkv = _tile(P, 512, 16)
    k5 = k_past.reshape(B, P, 2 * H // SUBLANES, SUBLANES, d)
    row = lambda b, g, j: (b, 0, g)
    return pl.pallas_call(
        functools.partial(_attn_sample_body, lam_init=lam_init),
        grid=(B, G, P // tkv),
        in_specs=[
            pl.BlockSpec((None, T, gw), row),
            pl.BlockSpec((None, tkv, None, SUBLANES, d), lambda b, g, j: (b, j, 2 * g, 0, 0)),
            pl.BlockSpec((None, tkv, None, SUBLANES, d), lambda b, g, j: (b, j, 2 * g + 1, 0, 0)),
            pl.BlockSpec((None, tkv, SUBLANES, d), lambda b, g, j: (b, j, g, 0)),
            pl.BlockSpec((None, tkv, SUBLANES, d), lambda b, g, j: (b, j, g, 1)),
            pl.BlockSpec((None, T, gw), row),
            pl.BlockSpec((None, T, gw), row),
            *_lam_specs(3),
            pl.BlockSpec((1, hd), lambda b, g, j: (0, 0)),
        ],
        out_specs=pl.BlockSpec((None, T, gw), row),
        out_shape=jax.ShapeDtypeStruct((B, T, DQ), BF16),
        scratch_shapes=[
            pltpu.VMEM((T, gw), BF16),
            pltpu.VMEM((HEAD_GROUP * 2 * T, 1), F32),
            pltpu.VMEM((HEAD_GROUP * 2 * T, 1), F32),
            pltpu.VMEM((HEAD_GROUP * 2 * T, hd), F32),
        ],
        compiler_params=_params("parallel", "parallel", "arbitrary"),
        name="diff_attn_sample",
    )(q, k5, k5, v_past, v_past, k_new, v_new, *lams, g_head.reshape(1, hd))


def _rope_tables(pos):
    half = ROT_DIM // 2
    inv = ROPE_THETA ** (-jnp.arange(0, ROT_DIM, 2, dtype=F32) / ROT_DIM)
    ang = pos.astype(F32)[:, None] * inv[None, :]
    cos, sin = jnp.cos(ang), jnp.sin(ang)
    T = pos.shape[0]
    ones = jnp.ones((T, DIFF_HEAD_DIM - ROT_DIM), F32)
    zeros_h = jnp.zeros((T, half), F32)
    zeros_r = jnp.zeros((T, DIFF_HEAD_DIM - ROT_DIM), F32)
    c = jnp.concatenate([cos, cos, ones], axis=1)
    sa = jnp.concatenate([-sin, zeros_h, zeros_r], axis=1)
    sb = jnp.concatenate([zeros_h, sin, zeros_r], axis=1)
    return c, sa, sb


def _trunk(x, mods, pos, s0, k_past, v_past, W):
    (g_norm, w_ffn_up, w_ffn_down, w_gla_in, w_gla_gate, b_gla_gate, g_gla_head, w_gla_out,
     g_kv, w_kv, w_dq, lams, g_diff_head, w_do, g_final) = W
    B, T, D = x.shape
    depth = g_norm.shape[0]
    n_a = w_gla_in.shape[0]
    tables = _rope_tables(pos)
    dk_all = w_gla_gate.shape[-1]
    dv_all = w_gla_out.shape[-2]
    d_qk = w_dq.shape[-1]

    h = x
    states = []
    k_sh = v_sh = None
    for l in range(depth):
        mod = mods[l]

        def ffn(h, s, i0):
            hn = _norm_call(h, g_norm[l, i0 // 3], mod, i0, i0 + 1)
            hid = _mm_swiglu(hn, w_ffn_up, (l, s))
            return _mm_down(hid, w_ffn_down, (l, s), h, mod, i0 + 2, 0.5)

        h = ffn(h, 0, 0)
        hn = _norm_call(h, g_norm[l, 1], mod, 3, 4)
        if l < n_a:
            p = _mm_plain(hn, w_gla_in, (l,), 0, 2 * dk_all + 2 * dv_all, "gla_in")
            w16 = w_gla_in[l][:, 2 * dk_all + 2 * dv_all:]
            la = _gla_gate_call(hn, w16, w_gla_gate[l], b_gla_gate[l])
            og, S = _gla_call(p, la, g_gla_head[l], s0, l, dk_all, dv_all)
            states.append(S)
            h = _mm_resid(og, w_gla_out, (l,), h, mod, 5, 1.0, "gla_out")
        else:
            jj = l - n_a
            lam_init = 0.8 - 0.6 * math.exp(-0.3 * l)
            q = _mm_rope(hn, w_dq, (jj,), d_qk, tables, "diff_q")
            lam_l = tuple(t[jj].reshape(1, DIFF_HEAD_DIM) for t in lams)
            if k_past is None:
                oa = _attn_prompt_call(q, k_sh, v_sh, lam_l, g_diff_head[jj], lam_init)
            else:
                oa = _attn_sample_call(q, k_past, v_past, k_sh, v_sh, lam_l, g_diff_head[jj], lam_init)
            h = _mm_resid(oa, w_do, (jj,), h, mod, 5, 1.0, "diff_out")
        h = ffn(h, 1, 6)
        if l == n_a - 1:
            xn = _norm_call(h, g_kv)
            k_sh = _mm_rope(xn, w_kv, (), d_qk, tables, "shared_k")
            v_sh = _mm_plain(xn, w_kv, (), d_qk, w_kv.shape[-1] - d_qk, "shared_v")
    y = _norm_call(h, g_final, out_dtype=F32)
    state = states[0][None] if len(states) == 1 else jnp.stack(states)
    return y, state, k_sh, v_sh


def kernel(x_prompt, x_sample, c_prompt, c_sample, state_gla, cache_k, cache_v, w_mod, b_mod, g_norm, w_ffn_up, w_ffn_down, w_gla_in, w_gla_gate, b_gla_gate, g_gla_head, w_gla_out, g_kv, w_kv, w_dq, lambda_q1, lambda_k1, lambda_q2, lambda_k2, g_diff_head, w_do, g_final):
    B, T, D = x_prompt.shape
    Bs, Ts, _ = x_sample.shape
    P = cache_k.shape[1]
    H = cache_k.shape[2]
    hd = 2 * DIFF_HEAD_DIM

    rows = B + Bs
    rows_pad = -(-rows // 16) * 16
    c_all = jnp.concatenate([c_prompt, c_sample, jnp.zeros((rows_pad - rows, D), F32)], axis=0)
    mod_all = _mod_call(c_all, w_mod, b_mod)
    L = w_mod.shape[0]
    mods_p = [mod_all[l, :B].reshape(B, N_MOD, 1, D) for l in range(L)]
    mods_s = [mod_all[l, B:rows].reshape(Bs, N_MOD, 1, D) for l in range(L)]

    W = (g_norm, w_ffn_up, w_ffn_down, w_gla_in, w_gla_gate, b_gla_gate, g_gla_head, w_gla_out,
         g_kv, w_kv, w_dq, (lambda_q1, lambda_k1, lambda_q2, lambda_k2), g_diff_head, w_do, g_final)

    y_p, s_p, k_p, v_p = _trunk(x_prompt, mods_p, jnp.arange(T), None, None, None, W)
    y_s, s_s, k_s, v_s = _trunk(x_sample, mods_s, P + jnp.arange(Ts), state_gla, cache_k, cache_v, W)
    return (y_p, y_s, s_p, s_s,
            k_p.reshape(B, T, H, 2, DIFF_HEAD_DIM), v_p.reshape(B, T, H, hd),
            k_s.reshape(Bs, Ts, H, 2, DIFF_HEAD_DIM), v_s.reshape(Bs, Ts, H, hd))
```

```python
import functools
import math

import jax
import jax.numpy as jnp
from jax import lax
from jax.experimental import pallas as pl
from jax.experimental.pallas import tpu as pltpu

F32 = jnp.float32
BF16 = jnp.bfloat16

CHUNK = 64
GLA_HEADS = 4
GLA_GATE_TEMP = 16.0
DIFF_HEAD_DIM = 128
ROT_DIM = DIFF_HEAD_DIM // 4
ROPE_THETA = 500000.0
N_MOD = 9
EPS = 1e-6
NEG = -1e30

LANE = 128
V7X_VMEM_BYTES = 64 * 1024 * 1024
VMEM_LIMIT = 56 * 1024 * 1024


def _params(*sem):
    return pltpu.CompilerParams(dimension_semantics=sem, vmem_limit_bytes=VMEM_LIMIT)


def _tile(n, target, align):
    if n <= target:
        return n
    t = (target // align) * align
    while t >= align:
        if n % t == 0:
            return t
        t -= align
    return n


def _row_tile(B, T, tm):
    if T >= tm:
        tt = _tile(T, tm, 16)
        return 1, tt
    bb = _tile(B, max(tm // T, 1), 1)
    return bb, T


def _dot(a, b):
    return jnp.dot(a, b, preferred_element_type=F32)


def _dot_nt(a, b):
    return lax.dot_general(a, b, (((1,), (1,)), ((), ())), preferred_element_type=F32)


def _dot_tn(a, b):
    return lax.dot_general(a, b, (((0,), (0,)), ((), ())), preferred_element_type=F32)


def _silu(x):
    return x * jax.nn.sigmoid(x)


def _mod_body(c_ref, w_ref, b_ref, o_ref):
    x = _silu(c_ref[...]).astype(BF16)
    o_ref[...] = _dot(x, w_ref[...].astype(BF16)) + b_ref[...]


def _mod_call(c, w_mod, b_mod):
    L, D, N = w_mod.shape
    R = c.shape[0]
    tn = _tile(N, 512, LANE)
    return pl.pallas_call(
        _mod_body,
        grid=(L, N // tn),
        in_specs=[
            pl.BlockSpec((R, D), lambda l, j: (0, 0)),
            pl.BlockSpec((None, D, tn), lambda l, j: (l, 0, j)),
            pl.BlockSpec((None, 1, tn), lambda l, j: (l, 0, j)),
        ],
        out_specs=pl.BlockSpec((None, R, tn), lambda l, j: (l, 0, j)),
        out_shape=jax.ShapeDtypeStruct((L, R, N), F32),
        compiler_params=_params("parallel", "parallel"),
        name="mod",
    )(c, w_mod, b_mod.reshape(L, 1, N))


def _norm_body(h_ref, g_ref, *rest, modulated):
    o_ref = rest[-1]
    x = h_ref[...]
    y = x * lax.rsqrt(jnp.mean(x * x, axis=-1, keepdims=True) + EPS) * g_ref[...]
    if modulated:
        sh_ref, sc_ref = rest[0], rest[1]
        y = y * (1 + sc_ref[...]) + sh_ref[...]
    o_ref[...] = y.astype(o_ref.dtype)


def _norm_call(h, g, mod=None, i_shift=None, i_scale=None, out_dtype=BF16):
    B, T, D = h.shape
    bb, tt = _row_tile(B, T, 256)
    nt = T // tt
    row = lambda i: (i // nt, i % nt, 0)
    in_specs = [pl.BlockSpec((bb, tt, D), row), pl.BlockSpec((1, 1, D), lambda i: (0, 0, 0))]
    args = [h, g.reshape(1, 1, D)]
    if mod is not None:
        for idx in (i_shift, i_scale):
            in_specs.append(pl.BlockSpec((bb, None, 1, D), lambda i, idx=idx: (i // nt, idx, 0, 0)))
            args.append(mod)
    return pl.pallas_call(
        functools.partial(_norm_body, modulated=mod is not None),
        grid=((B // bb) * nt,),
        in_specs=in_specs,
        out_specs=pl.BlockSpec((bb, tt, D), row),
        out_shape=jax.ShapeDtypeStruct((B, T, D), out_dtype),
        compiler_params=_params("parallel"),
        name="norm",
    )(*args)


def _x2d(x_ref):
    bb, tt, K = x_ref.shape
    return x_ref[...].reshape(bb * tt, K)


def _mm_plain_body(x_ref, w_ref, o_ref):
    y = _dot(_x2d(x_ref), w_ref[...].astype(BF16))
    o_ref[...] = y.reshape(o_ref.shape).astype(o_ref.dtype)


def _mm_plain_t_body(x_ref, wt_ref, o_ref):
    y = _dot_nt(_x2d(x_ref), wt_ref[...].astype(BF16))
    o_ref[...] = y.reshape(o_ref.shape).astype(o_ref.dtype)


def _mm_swiglu_body(x_ref, wa_ref, wb_ref, o_ref):
    x = _x2d(x_ref)
    a = _dot(x, wa_ref[...].astype(BF16))
    b = _dot(x, wb_ref[...].astype(BF16))
    o_ref[...] = (_silu(a) * b).reshape(o_ref.shape).astype(o_ref.dtype)


def _mm_resid_body(x_ref, w_ref, h_ref, g_ref, o_ref, *, scale):
    y = _dot(_x2d(x_ref), w_ref[...].astype(BF16))
    o_ref[...] = h_ref[...] + (scale * g_ref[...]) * y.reshape(o_ref.shape)


ROPE_ROW_CHUNKS = 4


def _mm_rope_body(x_ref, w_ref, c_ref, sa_ref, sb_ref, o_ref):
    bb, tt, tn = o_ref.shape
    d = DIFF_HEAD_DIM
    half = ROT_DIM // 2
    x = _x2d(x_ref)
    wb = w_ref[...].astype(BF16)
    rows = (bb * tt) // ROPE_ROW_CHUNKS
    for c in range(ROPE_ROW_CHUNKS):
        y = _dot(x[c * rows:(c + 1) * rows], wb)
        if bb == 1:
            dst = (slice(0, 1), slice(c * rows, (c + 1) * rows))
            tab = slice(c * rows, (c + 1) * rows)
            shape = (1, rows, d)
        else:
            nb = rows // tt
            dst = (slice(c * nb, (c + 1) * nb), slice(None))
            tab = slice(None)
            shape = (nb, tt, d)
        ct, sa, sb = c_ref[tab, :][None], sa_ref[tab, :][None], sb_ref[tab, :][None]
        for g in range(tn // d):
            yg = y[:, g * d:(g + 1) * d]
            up = pltpu.roll(yg, d - half, 1).reshape(shape)
            dn = pltpu.roll(yg, half, 1).reshape(shape)
            o_ref[dst + (slice(g * d, (g + 1) * d),)] = yg.reshape(shape) * ct + up * sa + dn * sb


def _wspec(w, prefix, tn, col_block0, transposed=False):
    lead = (None,) * len(prefix)
    if transposed:
        return pl.BlockSpec(lead + (tn, w.shape[-1]), lambda i, j: tuple(prefix) + (j + col_block0, 0))
    return pl.BlockSpec(lead + (w.shape[-2], tn), lambda i, j: tuple(prefix) + (0, j + col_block0))


def _mm_call(body, x, weights, n_cols, tn, out_dtype, extra_specs=(), extra_args=(), tm=1024, x_buffers=2,
             weights_transposed=False, name="mm"):
    B, T, K = x.shape
    bb, tt = _row_tile(B, T, tm)
    nt = T // tt
    assert n_cols % tn == 0
    row = lambda i, j: (i // nt, i % nt, 0)
    x_mode = {} if x_buffers == 2 else dict(pipeline_mode=pl.Buffered(x_buffers))
    in_specs = [pl.BlockSpec((bb, tt, K), row, **x_mode)]
    args = [x]
    for w, prefix, col0 in weights:
        assert col0 % tn == 0 and w.shape[-1 if weights_transposed else -2] == K
        in_specs.append(_wspec(w, prefix, tn, col0 // tn, weights_transposed))
        args.append(w)
    for mk in extra_specs:
        in_specs.append(mk(bb, tt, nt, tn))
    args.extend(extra_args)
    return pl.pallas_call(
        body,
        grid=((B // bb) * nt, n_cols // tn),
        in_specs=in_specs,
        out_specs=pl.BlockSpec((bb, tt, tn), lambda i, j: (i // nt, i % nt, j)),
        out_shape=jax.ShapeDtypeStruct((B, T, n_cols), out_dtype),
        compiler_params=_params("parallel", "arbitrary"),
        name=name,
    )(*args)


def _h_spec(bb, tt, nt, tn):
    return pl.BlockSpec((bb, tt, tn), lambda i, j: (i // nt, i % nt, j))


def _gate_spec(idx):
    def mk(bb, tt, nt, tn):
        return pl.BlockSpec((bb, None, 1, tn), lambda i, j: (i // nt, idx, 0, j))
    return mk


def _table_spec(bb, tt, nt, tn):
    return pl.BlockSpec((tt, DIFF_HEAD_DIM), lambda i, j: (i % nt, 0))


def _mm_plain(x, w, prefix, col0, n_cols, name):
    tn = _tile(n_cols, 512, LANE)
    return _mm_call(_mm_plain_body, x, [(w, prefix, col0)], n_cols, tn, F32, name=name)


def _mm_plain_t(x, wt, prefix, n_cols, name):
    tn = _tile(n_cols, 512, LANE)
    return _mm_call(_mm_plain_t_body, x, [(wt, prefix, 0)], n_cols, tn, F32, tm=2048, x_buffers=1,
                    weights_transposed=True, name=name)


def _mm_swiglu(x, w_up, prefix):
    dff = w_up.shape[-1] // 2
    tn = _tile(dff, 256, LANE)
    return _mm_call(_mm_swiglu_body, x, [(w_up, prefix, 0), (w_up, prefix, dff)], dff, tn, BF16,
                    tm=2048, x_buffers=1, name="ffn_up_swiglu")


def _mm_resid(x, w, prefix, h, mod, gate_idx, scale, name):
    n = w.shape[-1]
    tn = _tile(n, 512, LANE)
    return _mm_call(functools.partial(_mm_resid_body, scale=scale), x, [(w, prefix, 0)], n, tn, F32,
                    extra_specs=(_h_spec, _gate_spec(gate_idx)), extra_args=(h, mod), name=name)


def _mm_rope(x, w, prefix, n_cols, tables, name):
    tn = _tile(n_cols, 512, LANE)
    return _mm_call(_mm_rope_body, x, [(w, prefix, 0)], n_cols, tn, F32,
                    extra_specs=(_table_spec,) * 3, extra_args=tables, name=name)


def _mm_down(x, w, prefix, h, mod, gate_idx, scale):
    n = w.shape[-1]
    tn = _tile(n, 256, LANE)
    return _mm_call(functools.partial(_mm_resid_body, scale=scale), x, [(w, prefix, 0)], n, tn, F32,
                    extra_specs=(_h_spec, _gate_spec(gate_idx)), extra_args=(h, mod),
                    x_buffers=1, name="ffn_down")


def _gla_gate_body(x_ref, w16t_ref, wg_ref, bg_ref, o_ref):
    glr = _dot_nt(_x2d(x_ref), w16t_ref[...].astype(BF16))
    z = _dot(glr.astype(BF16), wg_ref[...].astype(BF16)) + bg_ref[...]
    o_ref[...] = (jax.nn.log_sigmoid(z) / GLA_GATE_TEMP).reshape(o_ref.shape)


def _gla_gate_call(x, w16t, w_gate, b_gate):
    B, T, K = x.shape
    R, N = w_gate.shape
    bb, tt = _row_tile(B, T, 512)
    nt = T // tt
    return pl.pallas_call(
        _gla_gate_body,
        grid=((B // bb) * nt,),
        in_specs=[
            pl.BlockSpec((bb, tt, K), lambda i: (i // nt, i % nt, 0)),
            pl.BlockSpec((R, K), lambda i: (0, 0)),
            pl.BlockSpec((R, N), lambda i: (0, 0)),
            pl.BlockSpec((1, N), lambda i: (0, 0)),
        ],
        out_specs=pl.BlockSpec((bb, tt, N), lambda i: (i // nt, i % nt, 0)),
        out_shape=jax.ShapeDtypeStruct((B, T, N), F32),
        compiler_params=_params("parallel"),
        name="gla_gate",
    )(x, w16t, w_gate, b_gate.reshape(1, N))


def _gla_body(q_ref, k_ref, v_ref, r_ref, la_ref, gh_ref, *rest, has_s0):
    if has_s0:
        s0_ref, o_ref, s_ref = rest
    else:
        o_ref, s_ref = rest
    C, dk_all = q_ref.shape
    n_heads, dk, dv = s_ref.shape

    @pl.when(pl.program_id(1) == 0)
    def _():
        if has_s0:
            s_ref[...] = s0_ref[...]
        else:
            s_ref[...] = jnp.zeros_like(s_ref)

    la = la_ref[...]
    la_hi = la.astype(BF16)
    rem = la - la_hi.astype(F32)
    la_mid = rem.astype(BF16)
    la_lo = (rem - la_mid.astype(F32)).astype(BF16)
    row = lax.broadcasted_iota(jnp.int32, (C, C), 0)
    col = lax.broadcasted_iota(jnp.int32, (C, C), 1)
    causal = row >= col
    tri = causal.astype(BF16)
    b = _dot(tri, la_hi) + _dot(tri, la_mid) + _dot(tri, la_lo)
    b_last = b[C - 1:C, :]

    q = q_ref[...] * (dk ** -0.5)
    k = k_ref[...]
    qe = (q * jnp.exp(b)).astype(BF16)
    ke = (k * jnp.exp(-b)).astype(BF16)
    kd = (k * jnp.exp(b_last - b)).astype(BF16)
    vb = v_ref[...].astype(BF16)

    decay_col = jnp.transpose(jnp.exp(jnp.broadcast_to(b_last, (LANE, dk_all))))[:, :1]

    for h in range(n_heads):
        kc = slice(h * dk, (h + 1) * dk)
        vc = slice(h * dv, (h + 1) * dv)
        a = jnp.where(causal, _dot_nt(qe[:, kc], ke[:, kc]), 0.0)
        s = s_ref[h]
        o = _dot(a.astype(BF16), vb[:, vc]) + _dot(qe[:, kc], s.astype(BF16))
        s_ref[h] = decay_col[kc] * s + _dot_tn(kd[:, kc], vb[:, vc])
        y = o * lax.rsqrt(jnp.mean(o * o, axis=-1, keepdims=True) + EPS) * gh_ref[...]
        o_ref[:, vc] = (y * _silu(r_ref[:, vc])).astype(o_ref.dtype)


def _gla_call(p, la, g_head, s0_all, layer, dk_all, dv_all):
    B, T, _ = p.shape
    H = GLA_HEADS
    dk, dv = dk_all // H, dv_all // H
    C = min(CHUNK, T)
    assert T % C == 0
    nC = T // C
    assert (2 * dk_all) % dv_all == 0
    i_v = (2 * dk_all) // dv_all
    in_specs = [
        pl.BlockSpec((None, C, dk_all), lambda b, n: (b, n, 0)),
        pl.BlockSpec((None, C, dk_all), lambda b, n: (b, n, 1)),
        pl.BlockSpec((None, C, dv_all), lambda b, n: (b, n, i_v)),
        pl.BlockSpec((None, C, dv_all), lambda b, n: (b, n, i_v + 1)),
        pl.BlockSpec((None, C, dk_all), lambda b, n: (b, n, 0)),
        pl.BlockSpec((1, dv), lambda b, n: (0, 0)),
    ]
    args = [p, p, p, p, la, g_head.reshape(1, dv)]
    if s0_all is not None:
        in_specs.append(pl.BlockSpec((None, None, H, dk, dv), lambda b, n: (layer, b, 0, 0, 0)))
        args.append(s0_all)
    return pl.pallas_call(
        functools.partial(_gla_body, has_s0=s0_all is not None),
        grid=(B, nC),
        in_specs=in_specs,
        out_specs=[
            pl.BlockSpec((None, C, dv_all), lambda b, n: (b, n, 0)),
            pl.BlockSpec((None, H, dk, dv), lambda b, n: (b, 0, 0, 0)),
        ],
        out_shape=[
            jax.ShapeDtypeStruct((B, T, dv_all), BF16),
            jax.ShapeDtypeStruct((B, H, dk, dv), F32),
        ],
        compiler_params=_params("parallel", "arbitrary"),
        name="gla",
    )(*args)


def _lambda(lq1, lk1, lq2, lk2, lam_init):
    s1 = jnp.sum(lq1[...] * lk1[...], axis=-1, keepdims=True)
    s2 = jnp.sum(lq2[...] * lk2[...], axis=-1, keepdims=True)
    return jnp.exp(s1) - jnp.exp(s2) + lam_init


def _head_norm(o, g, lam_init):
    return o * lax.rsqrt(jnp.mean(o * o, axis=-1, keepdims=True) + EPS) * g * (1.0 - lam_init)


def _attn_prompt_body(q_ref, k_ref, v_ref, lq1, lk1, lq2, lk2, gh_ref, o_ref,
                      kb_ref, vb_ref, *, lam_init, nq):
    tq = q_ref.shape[0]
    d = DIFF_HEAD_DIM
    qi = pl.program_id(2)

    @pl.when(qi == 0)
    def _():
        kb_ref[...] = k_ref[...].astype(BF16)
        vb_ref[...] = v_ref[...].astype(BF16)

    q = q_ref[...] * (d ** -0.5)
    q1 = q[:, :d].astype(BF16)
    q2 = q[:, d:].astype(BF16)
    lam = _lambda(lq1, lk1, lq2, lk2, lam_init)
    row = (lax.broadcasted_iota(jnp.int32, (2 * tq, tq), 0) % tq) // CHUNK
    col = lax.broadcasted_iota(jnp.int32, (2 * tq, tq), 1) // CHUNK
    visible = col <= row

    for i in range(nq):
        @pl.when(qi == i)
        def _(i=i):
            n_keys = (i + 1) * tq
            kb = kb_ref[:n_keys, :]
            s = jnp.concatenate([_dot_nt(q1, kb[:, :d]), _dot_nt(q2, kb[:, d:])], axis=0)
            s_diag = jnp.where(visible, s[:, n_keys - tq:], NEG)
            s = s_diag if i == 0 else jnp.concatenate([s[:, :n_keys - tq], s_diag], axis=1)
            e = jnp.exp(s - jnp.max(s, axis=-1, keepdims=True))
            inv = 1.0 / jnp.sum(e, axis=-1, keepdims=True)
            w = e[:tq] * inv[:tq] - e[tq:] * (lam * inv[tq:])
            o = _dot(w.astype(BF16), vb_ref[:n_keys, :])
            o_ref[...] = _head_norm(o, gh_ref[...], lam_init).astype(o_ref.dtype)


def _lam_specs(nd):
    zero = (0,) * 2
    return [pl.BlockSpec((1, DIFF_HEAD_DIM), lambda *a: zero) for _ in range(4)]


def _attn_prompt_call(q, k, v, lams, g_head, lam_init):
    B, T, DQ = q.shape
    hd = 2 * DIFF_HEAD_DIM
    H = DQ // hd
    tq = _tile(T, 512, CHUNK)
    nq = T // tq
    return pl.pallas_call(
        functools.partial(_attn_prompt_body, lam_init=lam_init, nq=nq),
        grid=(B, H, nq),
        in_specs=[
            pl.BlockSpec((None, tq, hd), lambda b, h, i: (b, i, h)),
            pl.BlockSpec((None, T, hd), lambda b, h, i: (b, 0, h)),
            pl.BlockSpec((None, T, hd), lambda b, h, i: (b, 0, h)),
            *_lam_specs(3),
            pl.BlockSpec((1, hd), lambda b, h, i: (0, 0)),
        ],
        out_specs=pl.BlockSpec((None, tq, hd), lambda b, h, i: (b, i, h)),
        out_shape=jax.ShapeDtypeStruct((B, T, DQ), BF16),
        scratch_shapes=[pltpu.VMEM((T, hd), BF16), pltpu.VMEM((T, hd), BF16)],
        compiler_params=_params("parallel", "parallel", "arbitrary"),
        name="diff_attn_prompt",
    )(q, k, v, *lams, g_head.reshape(1, hd))


SUBLANES = 8
HEAD_GROUP = SUBLANES


def _rows_at_sublane(ref, r):
    n, s, lanes = ref.shape
    return ref.reshape(n * s, lanes)[pl.ds(r, n, stride=s), :]


def _attn_sample_body(q_ref, ka_ref, kb_ref, vlo_ref, vhi_ref, kn_ref, vn_ref, lq1, lk1, lq2, lk2, gh_ref,
                      o_ref, qb_ref, m_ref, l_ref, acc_ref, *, lam_init):
    d = DIFF_HEAD_DIM
    hd = 2 * d
    tq = q_ref.shape[0]
    tkv = ka_ref.shape[0]
    j = pl.program_id(2)

    @pl.when(j == 0)
    def _():
        qb_ref[...] = (q_ref[...] * (d ** -0.5)).astype(BF16)
        m_ref[...] = jnp.full_like(m_ref, NEG)
        l_ref[...] = jnp.zeros_like(l_ref)
        acc_ref[...] = jnp.zeros_like(acc_ref)

    def cached(h):
        k_tile = (ka_ref, kb_ref)[(2 * h) // SUBLANES]
        r = (2 * h) % SUBLANES
        k1 = _rows_at_sublane(k_tile, r)
        k2 = _rows_at_sublane(k_tile, r + 1)
        v = jnp.concatenate([_rows_at_sublane(vlo_ref, h), _rows_at_sublane(vhi_ref, h)], axis=1)
        return k1, k2, v

    def fresh(h):
        return kn_ref[:, h * hd:h * hd + d], kn_ref[:, h * hd + d:(h + 1) * hd], vn_ref[:, h * hd:(h + 1) * hd]

    def process(get):
        scores, values = [], []
        for h in range(HEAD_GROUP):
            k1, k2, v = get(h)
            scores.append(_dot_nt(qb_ref[:, h * hd:h * hd + d], k1.astype(BF16)))
            scores.append(_dot_nt(qb_ref[:, h * hd + d:(h + 1) * hd], k2.astype(BF16)))
            values.append(v.astype(BF16))
        s = jnp.concatenate(scores, axis=0)
        m_prev = m_ref[...]
        m_new = jnp.maximum(m_prev, jnp.max(s, axis=-1, keepdims=True))
        alpha = jnp.exp(m_prev - m_new)
        p = jnp.exp(s - m_new)
        l_ref[...] = alpha * l_ref[...] + jnp.sum(p, axis=-1, keepdims=True)
        pb = p.astype(BF16)
        pv = jnp.concatenate([_dot(pb[h * 2 * tq:(h + 1) * 2 * tq], values[h]) for h in range(HEAD_GROUP)], axis=0)
        acc_ref[...] = alpha * acc_ref[...] + pv
        m_ref[...] = m_new

    process(cached)

    @pl.when(j == pl.num_programs(2) - 1)
    def _():
        process(fresh)
        lam = _lambda(lq1, lk1, lq2, lk2, lam_init)
        on = acc_ref[...] / l_ref[...]
        for h in range(HEAD_GROUP):
            o = on[h * 2 * tq:h * 2 * tq + tq] - lam * on[h * 2 * tq + tq:(h + 1) * 2 * tq]
            o_ref[:, h * hd:(h + 1) * hd] = _head_norm(o, gh_ref[...], lam_init).astype(o_ref.dtype)


def _attn_sample_call(q, k_past, v_past, k_new, v_new, lams, g_head, lam_init):
    B, T, DQ = q.shape
    P, H = k_past.shape[1], k_past.shape[2]
    d = DIFF_HEAD_DIM
    hd = 2 * d
    assert H % HEAD_GROUP == 0
    G = H // HEAD_GROUP
    gw = HEAD_GROUP * hd
    tkv = _tile(P, 512, 16)
    k5 = k_past.reshape(B, P, 2 * H // SUBLANES, SUBLANES, d)
    row = lambda b, g, j: (b, 0, g)
    return pl.pallas_call(
        functools.partial(_attn_sample_body, lam_init=lam_init),
        grid=(B, G, P // tkv),
        in_specs=[
            pl.BlockSpec((None, T, gw), row),
            pl.BlockSpec((None, tkv, None, SUBLANES, d), lambda b, g, j: (b, j, 2 * g, 0, 0)),
            pl.BlockSpec((None, tkv, None, SUBLANES, d), lambda b, g, j: (b, j, 2 * g + 1, 0, 0)),
            pl.BlockSpec((None, tkv, SUBLANES, d), lambda b, g, j: (b, j, g, 0)),
            pl.BlockSpec((None, tkv, SUBLANES, d), lambda b, g, j: (b, j, g, 1)),
            pl.BlockSpec((None, T, gw), row),
            pl.BlockSpec((None, T, gw), row),
            *_lam_specs(3),
            pl.BlockSpec((1, hd), lambda b, g, j: (0, 0)),
        ],
        out_specs=pl.BlockSpec((None, T, gw), row),
        out_shape=jax.ShapeDtypeStruct((B, T, DQ), BF16),
        scratch_shapes=[
            pltpu.VMEM((T, gw), BF16),
            pltpu.VMEM((HEAD_GROUP * 2 * T, 1), F32),
            pltpu.VMEM((HEAD_GROUP * 2 * T, 1), F32),
            pltpu.VMEM((HEAD_GROUP * 2 * T, hd), F32),
        ],
        compiler_params=_params("parallel", "parallel", "arbitrary"),
        name="diff_attn_sample",
    )(q, k5, k5, v_past, v_past, k_new, v_new, *lams, g_head.reshape(1, hd))


def _rope_tables(pos):
    half = ROT_DIM // 2
    inv = ROPE_THETA ** (-jnp.arange(0, ROT_DIM, 2, dtype=F32) / ROT_DIM)
    ang = pos.astype(F32)[:, None] * inv[None, :]
    cos, sin = jnp.cos(ang), jnp.sin(ang)
    T = pos.shape[0]
    ones = jnp.ones((T, DIFF_HEAD_DIM - ROT_DIM), F32)
    zeros_h = jnp.zeros((T, half), F32)
    zeros_r = jnp.zeros((T, DIFF_HEAD_DIM - ROT_DIM), F32)
    c = jnp.concatenate([cos, cos, ones], axis=1)
    sa = jnp.concatenate([-sin, zeros_h, zeros_r], axis=1)
    sb = jnp.concatenate([zeros_h, sin, zeros_r], axis=1)
    return c, sa, sb


def _trunk(x, mods, pos, s0, k_past, v_past, W):
    (g_norm, w_ffn_up, w_ffn_down, w_gla_in_t, w_gla_gate, b_gla_gate, g_gla_head, w_gla_out,
     g_kv, w_kv, w_dq, lams, g_diff_head, w_do, g_final) = W
    B, T, D = x.shape
    depth = g_norm.shape[0]
    n_a = w_gla_in_t.shape[0]
    tables = _rope_tables(pos)
    dk_all = w_gla_gate.shape[-1]
    dv_all = w_gla_out.shape[-2]
    d_qk = w_dq.shape[-1]

    h = x
    states = []
    k_sh = v_sh = None
    for l in range(depth):
        mod = mods[l]

        def ffn(h, s, i0):
            hn = _norm_call(h, g_norm[l, i0 // 3], mod, i0, i0 + 1)
            hid = _mm_swiglu(hn, w_ffn_up, (l, s))
            return _mm_down(hid, w_ffn_down, (l, s), h, mod, i0 + 2, 0.5)

        h = ffn(h, 0, 0)
        hn = _norm_call(h, g_norm[l, 1], mod, 3, 4)
        if l < n_a:
            n_qkvr = 2 * dk_all + 2 * dv_all
            p = _mm_plain_t(hn, w_gla_in_t, (l,), n_qkvr, "gla_in")
            la = _gla_gate_call(hn, w_gla_in_t[l, n_qkvr:], w_gla_gate[l], b_gla_gate[l])
            og, S = _gla_call(p, la, g_gla_head[l], s0, l, dk_all, dv_all)
            states.append(S)
            h = _mm_resid(og, w_gla_out, (l,), h, mod, 5, 1.0, "gla_out")
        else:
            jj = l - n_a
            lam_init = 0.8 - 0.6 * math.exp(-0.3 * l)
            q = _mm_rope(hn, w_dq, (jj,), d_qk, tables, "diff_q")
            lam_l = tuple(t[jj].reshape(1, DIFF_HEAD_DIM) for t in lams)
            if k_past is None:
                oa = _attn_prompt_call(q, k_sh, v_sh, lam_l, g_diff_head[jj], lam_init)
            else:
                oa = _attn_sample_call(q, k_past, v_past, k_sh, v_sh, lam_l, g_diff_head[jj], lam_init)
            h = _mm_resid(oa, w_do, (jj,), h, mod, 5, 1.0, "diff_out")
        h = ffn(h, 1, 6)
        if l == n_a - 1:
            xn = _norm_call(h, g_kv)
            k_sh = _mm_rope(xn, w_kv, (), d_qk, tables, "shared_k")
            v_sh = _mm_plain(xn, w_kv, (), d_qk, w_kv.shape[-1] - d_qk, "shared_v")
    y = _norm_call(h, g_final, out_dtype=F32)
    state = states[0][None] if len(states) == 1 else jnp.stack(states)
    return y, state, k_sh, v_sh


def kernel(x_prompt, x_sample, c_prompt, c_sample, state_gla, cache_k, cache_v, w_mod, b_mod, g_norm, w_ffn_up, w_ffn_down, w_gla_in, w_gla_gate, b_gla_gate, g_gla_head, w_gla_out, g_kv, w_kv, w_dq, lambda_q1, lambda_k1, lambda_q2, lambda_k2, g_diff_head, w_do, g_final):
    B, T, D = x_prompt.shape
    Bs, Ts, _ = x_sample.shape
    P = cache_k.shape[1]
    H = cache_k.shape[2]
    hd = 2 * DIFF_HEAD_DIM

    rows = B + Bs
    rows_pad = -(-rows // 16) * 16
    c_all = jnp.concatenate([c_prompt, c_sample, jnp.zeros((rows_pad - rows, D), F32)], axis=0)
    mod_all = _mod_call(c_all, w_mod, b_mod)
    L = w_mod.shape[0]
    mods_p = [mod_all[l, :B].reshape(B, N_MOD, 1, D) for l in range(L)]
    mods_s = [mod_all[l, B:rows].reshape(Bs, N_MOD, 1, D) for l in range(L)]

    w_gla_in_t = jnp.swapaxes(w_gla_in, 1, 2)
    W = (g_norm, w_ffn_up, w_ffn_down, w_gla_in_t, w_gla_gate, b_gla_gate, g_gla_head, w_gla_out,
         g_kv, w_kv, w_dq, (lambda_q1, lambda_k1, lambda_q2, lambda_k2), g_diff_head, w_do, g_final)

    y_p, s_p, k_p, v_p = _trunk(x_prompt, mods_p, jnp.arange(T), None, None, None, W)
    y_s, s_s, k_s, v_s = _trunk(x_sample, mods_s, P + jnp.arange(Ts), state_gla, cache_k, cache_v, W)
    return (y_p, y_s, s_p, s_s,
            k_p.reshape(B, T, H, 2, DIFF_HEAD_DIM), v_p.reshape(B, T, H, hd),
            k_s.reshape(Bs, Ts, H, 2, DIFF_HEAD_DIM), v_s.reshape(Bs, Ts, H, hd))
```

```python
import functools
import math

import jax
import jax.numpy as jnp
from jax import lax
from jax.experimental import pallas as pl
from jax.experimental.pallas import tpu as pltpu

F32 = jnp.float32
BF16 = jnp.bfloat16

CHUNK = 64
GLA_HEADS = 4
GLA_GATE_TEMP = 16.0
DIFF_HEAD_DIM = 128
ROT_DIM = DIFF_HEAD_DIM // 4
ROPE_THETA = 500000.0
N_MOD = 9
EPS = 1e-6
NEG = -1e30

LANE = 128
V7X_VMEM_BYTES = 64 * 1024 * 1024
VMEM_LIMIT = 56 * 1024 * 1024


def _params(*sem):
    return pltpu.CompilerParams(dimension_semantics=sem, vmem_limit_bytes=VMEM_LIMIT)


def _tile(n, target, align):
    if n <= target:
        return n
    t = (target // align) * align
    while t >= align:
        if n % t == 0:
            return t
        t -= align
    return n


def _row_tile(B, T, tm):
    if T >= tm:
        tt = _tile(T, tm, 16)
        return 1, tt
    bb = _tile(B, max(tm // T, 1), 1)
    return bb, T


def _dot(a, b):
    return jnp.dot(a, b, preferred_element_type=F32)


def _dot_nt(a, b):
    return lax.dot_general(a, b, (((1,), (1,)), ((), ())), preferred_element_type=F32)


def _dot_tn(a, b):
    return lax.dot_general(a, b, (((0,), (0,)), ((), ())), preferred_element_type=F32)


def _silu(x):
    return x * jax.nn.sigmoid(x)


def _mod_body(c_ref, w_ref, b_ref, o_ref):
    x = _silu(c_ref[...]).astype(BF16)
    o_ref[...] = _dot(x, w_ref[...].astype(BF16)) + b_ref[...]


def _mod_call(c, w_mod, b_mod):
    L, D, N = w_mod.shape
    R = c.shape[0]
    tn = _tile(N, 512, LANE)
    return pl.pallas_call(
        _mod_body,
        grid=(L, N // tn),
        in_specs=[
            pl.BlockSpec((R, D), lambda l, j: (0, 0)),
            pl.BlockSpec((None, D, tn), lambda l, j: (l, 0, j)),
            pl.BlockSpec((None, 1, tn), lambda l, j: (l, 0, j)),
        ],
        out_specs=pl.BlockSpec((None, R, tn), lambda l, j: (l, 0, j)),
        out_shape=jax.ShapeDtypeStruct((L, R, N), F32),
        compiler_params=_params("parallel", "parallel"),
        name="mod",
    )(c, w_mod, b_mod.reshape(L, 1, N))


NORM_ROWS = 16


def _norm_body(h_ref, g_ref, *rest, modulated):
    o_ref = rest[-1]
    bb, tt, _ = h_ref.shape
    g = g_ref[0]
    for b in range(bb):
        if modulated:
            sh_ref, sc_ref = rest[0], rest[1]
            shift, gain = sh_ref[b], 1 + sc_ref[b]
        for r in range(0, tt, NORM_ROWS):
            x = h_ref[b, r:r + NORM_ROWS, :]
            y = x * lax.rsqrt(jnp.mean(x * x, axis=-1, keepdims=True) + EPS) * g
            if modulated:
                y = y * gain + shift
            o_ref[b, r:r + NORM_ROWS, :] = y.astype(o_ref.dtype)


def _norm_call(h, g, mod=None, i_shift=None, i_scale=None, out_dtype=BF16):
    B, T, D = h.shape
    bb, tt = _row_tile(B, T, 512)
    nt = T // tt
    row = lambda i: (i // nt, i % nt, 0)
    in_specs = [pl.BlockSpec((bb, tt, D), row), pl.BlockSpec((1, 1, D), lambda i: (0, 0, 0))]
    args = [h, g.reshape(1, 1, D)]
    if mod is not None:
        for idx in (i_shift, i_scale):
            in_specs.append(pl.BlockSpec((bb, None, 1, D), lambda i, idx=idx: (i // nt, idx, 0, 0)))
            args.append(mod)
    return pl.pallas_call(
        functools.partial(_norm_body, modulated=mod is not None),
        grid=((B // bb) * nt,),
        in_specs=in_specs,
        out_specs=pl.BlockSpec((bb, tt, D), row),
        out_shape=jax.ShapeDtypeStruct((B, T, D), out_dtype),
        compiler_params=_params("parallel"),
        name="norm",
    )(*args)


def _x2d(x_ref):
    bb, tt, K = x_ref.shape
    return x_ref[...].reshape(bb * tt, K)


def _mm_plain_body(x_ref, w_ref, o_ref):
    y = _dot(_x2d(x_ref), w_ref[...].astype(BF16))
    o_ref[...] = y.reshape(o_ref.shape).astype(o_ref.dtype)


def _mm_plain_t_body(x_ref, wt_ref, o_ref):
    y = _dot_nt(_x2d(x_ref), wt_ref[...].astype(BF16))
    o_ref[...] = y.reshape(o_ref.shape).astype(o_ref.dtype)


def _mm_swiglu_body(x_ref, wa_ref, wb_ref, o_ref):
    x = _x2d(x_ref)
    a = _dot(x, wa_ref[...].astype(BF16))
    b = _dot(x, wb_ref[...].astype(BF16))
    o_ref[...] = (_silu(a) * b).reshape(o_ref.shape).astype(o_ref.dtype)


def _mm_resid_body(x_ref, w_ref, h_ref, g_ref, o_ref, *, scale):
    y = _dot(_x2d(x_ref), w_ref[...].astype(BF16))
    o_ref[...] = h_ref[...] + (scale * g_ref[...]) * y.reshape(o_ref.shape)


ROPE_ROW_CHUNKS = 4


def _mm_rope_body(x_ref, w_ref, c_ref, sa_ref, sb_ref, o_ref):
    bb, tt, tn = o_ref.shape
    d = DIFF_HEAD_DIM
    half = ROT_DIM // 2
    x = _x2d(x_ref)
    wb = w_ref[...].astype(BF16)
    rows = (bb * tt) // ROPE_ROW_CHUNKS
    for c in range(ROPE_ROW_CHUNKS):
        y = _dot(x[c * rows:(c + 1) * rows], wb)
        if bb == 1:
            dst = (slice(0, 1), slice(c * rows, (c + 1) * rows))
            tab = slice(c * rows, (c + 1) * rows)
            shape = (1, rows, d)
        else:
            nb = rows // tt
            dst = (slice(c * nb, (c + 1) * nb), slice(None))
            tab = slice(None)
            shape = (nb, tt, d)
        ct, sa, sb = c_ref[tab, :][None], sa_ref[tab, :][None], sb_ref[tab, :][None]
        for g in range(tn // d):
            yg = y[:, g * d:(g + 1) * d]
            up = pltpu.roll(yg, d - half, 1).reshape(shape)
            dn = pltpu.roll(yg, half, 1).reshape(shape)
            o_ref[dst + (slice(g * d, (g + 1) * d),)] = yg.reshape(shape) * ct + up * sa + dn * sb


def _wspec(w, prefix, tn, col_block0, transposed=False):
    lead = (None,) * len(prefix)
    if transposed:
        return pl.BlockSpec(lead + (tn, w.shape[-1]), lambda i, j: tuple(prefix) + (j + col_block0, 0))
    return pl.BlockSpec(lead + (w.shape[-2], tn), lambda i, j: tuple(prefix) + (0, j + col_block0))


def _mm_call(body, x, weights, n_cols, tn, out_dtype, extra_specs=(), extra_args=(), tm=1024, x_buffers=2,
             weights_transposed=False, name="mm"):
    B, T, K = x.shape
    bb, tt = _row_tile(B, T, tm)
    nt = T // tt
    assert n_cols % tn == 0
    row = lambda i, j: (i // nt, i % nt, 0)
    x_mode = {} if x_buffers == 2 else dict(pipeline_mode=pl.Buffered(x_buffers))
    in_specs = [pl.BlockSpec((bb, tt, K), row, **x_mode)]
    args = [x]
    for w, prefix, col0 in weights:
        assert col0 % tn == 0 and w.shape[-1 if weights_transposed else -2] == K
        in_specs.append(_wspec(w, prefix, tn, col0 // tn, weights_transposed))
        args.append(w)
    for mk in extra_specs:
        in_specs.append(mk(bb, tt, nt, tn))
    args.extend(extra_args)
    return pl.pallas_call(
        body,
        grid=((B // bb) * nt, n_cols // tn),
        in_specs=in_specs,
        out_specs=pl.BlockSpec((bb, tt, tn), lambda i, j: (i // nt, i % nt, j)),
        out_shape=jax.ShapeDtypeStruct((B, T, n_cols), out_dtype),
        compiler_params=_params("parallel", "arbitrary"),
        name=name,
    )(*args)


def _h_spec(bb, tt, nt, tn):
    return pl.BlockSpec((bb, tt, tn), lambda i, j: (i // nt, i % nt, j))


def _gate_spec(idx):
    def mk(bb, tt, nt, tn):
        return pl.BlockSpec((bb, None, 1, tn), lambda i, j: (i // nt, idx, 0, j))
    return mk


def _table_spec(bb, tt, nt, tn):
    return pl.BlockSpec((tt, DIFF_HEAD_DIM), lambda i, j: (i % nt, 0))


def _mm_plain(x, w, prefix, col0, n_cols, name):
    tn = _tile(n_cols, 512, LANE)
    return _mm_call(_mm_plain_body, x, [(w, prefix, col0)], n_cols, tn, F32, name=name)


def _mm_plain_t(x, wt, prefix, n_cols, name):
    tn = _tile(n_cols, 512, LANE)
    return _mm_call(_mm_plain_t_body, x, [(wt, prefix, 0)], n_cols, tn, F32, tm=2048, x_buffers=1,
                    weights_transposed=True, name=name)


def _mm_swiglu(x, w_up, prefix):
    dff = w_up.shape[-1] // 2
    tn = _tile(dff, 256, LANE)
    return _mm_call(_mm_swiglu_body, x, [(w_up, prefix, 0), (w_up, prefix, dff)], dff, tn, BF16,
                    tm=2048, x_buffers=1, name="ffn_up_swiglu")


def _mm_resid(x, w, prefix, h, mod, gate_idx, scale, name):
    n = w.shape[-1]
    tn = _tile(n, 512, LANE)
    return _mm_call(functools.partial(_mm_resid_body, scale=scale), x, [(w, prefix, 0)], n, tn, F32,
                    extra_specs=(_h_spec, _gate_spec(gate_idx)), extra_args=(h, mod), name=name)


def _mm_rope(x, w, prefix, n_cols, tables, name):
    tn = _tile(n_cols, 512, LANE)
    return _mm_call(_mm_rope_body, x, [(w, prefix, 0)], n_cols, tn, F32,
                    extra_specs=(_table_spec,) * 3, extra_args=tables, name=name)


def _mm_down(x, w, prefix, h, mod, gate_idx, scale):
    n = w.shape[-1]
    tn = _tile(n, 256, LANE)
    return _mm_call(functools.partial(_mm_resid_body, scale=scale), x, [(w, prefix, 0)], n, tn, F32,
                    extra_specs=(_h_spec, _gate_spec(gate_idx)), extra_args=(h, mod),
                    x_buffers=1, name="ffn_down")


def _gla_gate_body(x_ref, w16t_ref, wg_ref, bg_ref, o_ref):
    glr = _dot_nt(_x2d(x_ref), w16t_ref[...].astype(BF16))
    z = _dot(glr.astype(BF16), wg_ref[...].astype(BF16)) + bg_ref[...]
    o_ref[...] = (jax.nn.log_sigmoid(z) / GLA_GATE_TEMP).reshape(o_ref.shape)


def _gla_gate_call(x, w16t, w_gate, b_gate):
    B, T, K = x.shape
    R, N = w_gate.shape
    bb, tt = _row_tile(B, T, 512)
    nt = T // tt
    return pl.pallas_call(
        _gla_gate_body,
        grid=((B // bb) * nt,),
        in_specs=[
            pl.BlockSpec((bb, tt, K), lambda i: (i // nt, i % nt, 0)),
            pl.BlockSpec((R, K), lambda i: (0, 0)),
            pl.BlockSpec((R, N), lambda i: (0, 0)),
            pl.BlockSpec((1, N), lambda i: (0, 0)),
        ],
        out_specs=pl.BlockSpec((bb, tt, N), lambda i: (i // nt, i % nt, 0)),
        out_shape=jax.ShapeDtypeStruct((B, T, N), F32),
        compiler_params=_params("parallel"),
        name="gla_gate",
    )(x, w16t, w_gate, b_gate.reshape(1, N))


def _gla_body(q_ref, k_ref, v_ref, r_ref, la_ref, gh_ref, *rest, has_s0):
    if has_s0:
        s0_ref, o_ref, s_ref = rest
    else:
        o_ref, s_ref = rest
    C, dk_all = q_ref.shape
    n_heads, dk, dv = s_ref.shape

    @pl.when(pl.program_id(1) == 0)
    def _():
        if has_s0:
            s_ref[...] = s0_ref[...]
        else:
            s_ref[...] = jnp.zeros_like(s_ref)

    la = la_ref[...]
    la_hi = la.astype(BF16)
    rem = la - la_hi.astype(F32)
    la_mid = rem.astype(BF16)
    la_lo = (rem - la_mid.astype(F32)).astype(BF16)
    row = lax.broadcasted_iota(jnp.int32, (C, C), 0)
    col = lax.broadcasted_iota(jnp.int32, (C, C), 1)
    causal = row >= col
    tri = causal.astype(BF16)
    b = _dot(tri, la_hi) + _dot(tri, la_mid) + _dot(tri, la_lo)
    b_last = b[C - 1:C, :]

    q = q_ref[...] * (dk ** -0.5)
    k = k_ref[...]
    qe = (q * jnp.exp(b)).astype(BF16)
    ke = (k * jnp.exp(-b)).astype(BF16)
    kd = (k * jnp.exp(b_last - b)).astype(BF16)
    vb = v_ref[...].astype(BF16)

    decay_col = jnp.transpose(jnp.exp(jnp.broadcast_to(b_last, (LANE, dk_all))))[:, :1]

    for h in range(n_heads):
        kc = slice(h * dk, (h + 1) * dk)
        vc = slice(h * dv, (h + 1) * dv)
        a = jnp.where(causal, _dot_nt(qe[:, kc], ke[:, kc]), 0.0)
        s = s_ref[h]
        o = _dot(a.astype(BF16), vb[:, vc]) + _dot(qe[:, kc], s.astype(BF16))
        s_ref[h] = decay_col[kc] * s + _dot_tn(kd[:, kc], vb[:, vc])
        y = o * lax.rsqrt(jnp.mean(o * o, axis=-1, keepdims=True) + EPS) * gh_ref[...]
        o_ref[:, vc] = (y * _silu(r_ref[:, vc])).astype(o_ref.dtype)


def _gla_call(p, la, g_head, s0_all, layer, dk_all, dv_all):
    B, T, _ = p.shape
    H = GLA_HEADS
    dk, dv = dk_all // H, dv_all // H
    C = min(CHUNK, T)
    assert T % C == 0
    nC = T // C
    assert (2 * dk_all) % dv_all == 0
    i_v = (2 * dk_all) // dv_all
    in_specs = [
        pl.BlockSpec((None, C, dk_all), lambda b, n: (b, n, 0)),
        pl.BlockSpec((None, C, dk_all), lambda b, n: (b, n, 1)),
        pl.BlockSpec((None, C, dv_all), lambda b, n: (b, n, i_v)),
        pl.BlockSpec((None, C, dv_all), lambda b, n: (b, n, i_v + 1)),
        pl.BlockSpec((None, C, dk_all), lambda b, n: (b, n, 0)),
        pl.BlockSpec((1, dv), lambda b, n: (0, 0)),
    ]
    args = [p, p, p, p, la, g_head.reshape(1, dv)]
    if s0_all is not None:
        in_specs.append(pl.BlockSpec((None, None, H, dk, dv), lambda b, n: (layer, b, 0, 0, 0)))
        args.append(s0_all)
    return pl.pallas_call(
        functools.partial(_gla_body, has_s0=s0_all is not None),
        grid=(B, nC),
        in_specs=in_specs,
        out_specs=[
            pl.BlockSpec((None, C, dv_all), lambda b, n: (b, n, 0)),
            pl.BlockSpec((None, H, dk, dv), lambda b, n: (b, 0, 0, 0)),
        ],
        out_shape=[
            jax.ShapeDtypeStruct((B, T, dv_all), BF16),
            jax.ShapeDtypeStruct((B, H, dk, dv), F32),
        ],
        compiler_params=_params("parallel", "arbitrary"),
        name="gla",
    )(*args)


def _lambda(lq1, lk1, lq2, lk2, lam_init):
    s1 = jnp.sum(lq1[...] * lk1[...], axis=-1, keepdims=True)
    s2 = jnp.sum(lq2[...] * lk2[...], axis=-1, keepdims=True)
    return jnp.exp(s1) - jnp.exp(s2) + lam_init


def _head_norm(o, g, lam_init):
    return o * lax.rsqrt(jnp.mean(o * o, axis=-1, keepdims=True) + EPS) * g * (1.0 - lam_init)


def _attn_prompt_body(q_ref, k_ref, v_ref, lq1, lk1, lq2, lk2, gh_ref, o_ref,
                      kb_ref, vb_ref, *, lam_init, nq):
    tq = q_ref.shape[0]
    d = DIFF_HEAD_DIM
    qi = pl.program_id(2)

    @pl.when(qi == 0)
    def _():
        kb_ref[...] = k_ref[...].astype(BF16)
        vb_ref[...] = v_ref[...].astype(BF16)

    q = q_ref[...] * (d ** -0.5)
    q1 = q[:, :d].astype(BF16)
    q2 = q[:, d:].astype(BF16)
    lam = _lambda(lq1, lk1, lq2, lk2, lam_init)
    row = (lax.broadcasted_iota(jnp.int32, (2 * tq, tq), 0) % tq) // CHUNK
    col = lax.broadcasted_iota(jnp.int32, (2 * tq, tq), 1) // CHUNK
    visible = col <= row

    for i in range(nq):
        @pl.when(qi == i)
        def _(i=i):
            n_keys = (i + 1) * tq
            kb = kb_ref[:n_keys, :]
            s = jnp.concatenate([_dot_nt(q1, kb[:, :d]), _dot_nt(q2, kb[:, d:])], axis=0)
            s_diag = jnp.where(visible, s[:, n_keys - tq:], NEG)
            s = s_diag if i == 0 else jnp.concatenate([s[:, :n_keys - tq], s_diag], axis=1)
            e = jnp.exp(s - jnp.max(s, axis=-1, keepdims=True))
            inv = 1.0 / jnp.sum(e, axis=-1, keepdims=True)
            w = e[:tq] * inv[:tq] - e[tq:] * (lam * inv[tq:])
            o = _dot(w.astype(BF16), vb_ref[:n_keys, :])
            o_ref[...] = _head_norm(o, gh_ref[...], lam_init).astype(o_ref.dtype)


def _lam_specs(nd):
    zero = (0,) * 2
    return [pl.BlockSpec((1, DIFF_HEAD_DIM), lambda *a: zero) for _ in range(4)]


def _attn_prompt_call(q, k, v, lams, g_head, lam_init):
    B, T, DQ = q.shape
    hd = 2 * DIFF_HEAD_DIM
    H = DQ // hd
    tq = _tile(T, 512, CHUNK)
    nq = T // tq
    return pl.pallas_call(
        functools.partial(_attn_prompt_body, lam_init=lam_init, nq=nq),
        grid=(B, H, nq),
        in_specs=[
            pl.BlockSpec((None, tq, hd), lambda b, h, i: (b, i, h)),
            pl.BlockSpec((None, T, hd), lambda b, h, i: (b, 0, h)),
            pl.BlockSpec((None, T, hd), lambda b, h, i: (b, 0, h)),
            *_lam_specs(3),
            pl.BlockSpec((1, hd), lambda b, h, i: (0, 0)),
        ],
        out_specs=pl.BlockSpec((None, tq, hd), lambda b, h, i: (b, i, h)),
        out_shape=jax.ShapeDtypeStruct((B, T, DQ), BF16),
        scratch_shapes=[pltpu.VMEM((T, hd), BF16), pltpu.VMEM((T, hd), BF16)],
        compiler_params=_params("parallel", "parallel", "arbitrary"),
        name="diff_attn_prompt",
    )(q, k, v, *lams, g_head.reshape(1, hd))


SUBLANES = 8
HEAD_GROUP = SUBLANES


def _rows_at_sublane(ref, r):
    n, s, lanes = ref.shape
    return ref.reshape(n * s, lanes)[pl.ds(r, n, stride=s), :]


def _attn_sample_body(q_ref, ka_ref, kb_ref, vlo_ref, vhi_ref, kn_ref, vn_ref, lq1, lk1, lq2, lk2, gh_ref,
                      o_ref, qb_ref, m_ref, l_ref, acc_ref, *, lam_init):
    d = DIFF_HEAD_DIM
    hd = 2 * d
    tq = q_ref.shape[0]
    tkv = ka_ref.shape[0]
    j = pl.program_id(2)

    @pl.when(j == 0)
    def _():
        q = q_ref[...] * (d ** -0.5)
        zeros = jnp.zeros((tq, d), F32)
        for h in range(HEAD_GROUP):
            top = jnp.concatenate([q[:, h * hd:h * hd + d], zeros], axis=1)
            bottom = jnp.concatenate([zeros, q[:, h * hd + d:(h + 1) * hd]], axis=1)
            qb_ref[h] = jnp.concatenate([top, bottom], axis=0).astype(BF16)
        m_ref[...] = jnp.full_like(m_ref, NEG)
        l_ref[...] = jnp.zeros_like(l_ref)
        acc_ref[...] = jnp.zeros_like(acc_ref)

    def cached(h):
        k_tile = (ka_ref, kb_ref)[(2 * h) // SUBLANES]
        r = (2 * h) % SUBLANES
        k = jnp.concatenate([_rows_at_sublane(k_tile, r), _rows_at_sublane(k_tile, r + 1)], axis=1)
        v = jnp.concatenate([_rows_at_sublane(vlo_ref, h), _rows_at_sublane(vhi_ref, h)], axis=1)
        return k, v

    def fresh(h):
        return kn_ref[:, h * hd:(h + 1) * hd], vn_ref[:, h * hd:(h + 1) * hd]

    def process(get):
        scores, values = [], []
        for h in range(HEAD_GROUP):
            k, v = get(h)
            scores.append(_dot_nt(qb_ref[h], k.astype(BF16)))
            values.append(v.astype(BF16))
        s = jnp.concatenate(scores, axis=0)
        m_prev = m_ref[...]
        m_new = jnp.maximum(m_prev, jnp.max(s, axis=-1, keepdims=True))
        alpha = jnp.exp(m_prev - m_new)
        p = jnp.exp(s - m_new)
        l_ref[...] = alpha * l_ref[...] + jnp.sum(p, axis=-1, keepdims=True)
        pb = p.astype(BF16)
        pv = jnp.concatenate([_dot(pb[h * 2 * tq:(h + 1) * 2 * tq], values[h]) for h in range(HEAD_GROUP)], axis=0)
        acc_ref[...] = alpha * acc_ref[...] + pv
        m_ref[...] = m_new

    process(cached)

    @pl.when(j == pl.num_programs(2) - 1)
    def _():
        process(fresh)
        lam = _lambda(lq1, lk1, lq2, lk2, lam_init)
        on = acc_ref[...] / l_ref[...]
        for h in range(HEAD_GROUP):
            o = on[h * 2 * tq:h * 2 * tq + tq] - lam * on[h * 2 * tq + tq:(h + 1) * 2 * tq]
            o_ref[:, h * hd:(h + 1) * hd] = _head_norm(o, gh_ref[...], lam_init).astype(o_ref.dtype)


def _attn_sample_call(q, k_past, v_past, k_new, v_new, lams, g_head, lam_init):
    B, T, DQ = q.shape
    P, H = k_past.shape[1], k_past.shape[2]
    d = DIFF_HEAD_DIM
    hd = 2 * d
    assert H % HEAD_GROUP == 0
    G = H // HEAD_GROUP
    gw = HEAD_GROUP * hd
    tkv = _tile(P, 1024, 16)
    k5 = k_past.reshape(B, P, 2 * H // SUBLANES, SUBLANES, d)
    row = lambda b, g, j: (b, 0, g)
    return pl.pallas_call(
        functools.partial(_attn_sample_body, lam_init=lam_init),
        grid=(B, G, P // tkv),
        in_specs=[
            pl.BlockSpec((None, T, gw), row),
            pl.BlockSpec((None, tkv, None, SUBLANES, d), lambda b, g, j: (b, j, 2 * g, 0, 0)),
            pl.BlockSpec((None, tkv, None, SUBLANES, d), lambda b, g, j: (b, j, 2 * g + 1, 0, 0)),
            pl.BlockSpec((None, tkv, SUBLANES, d), lambda b, g, j: (b, j, g, 0)),
            pl.BlockSpec((None, tkv, SUBLANES, d), lambda b, g, j: (b, j, g, 1)),
            pl.BlockSpec((None, T, gw), row),
            pl.BlockSpec((None, T, gw), row),
            *_lam_specs(3),
            pl.BlockSpec((1, hd), lambda b, g, j: (0, 0)),
        ],
        out_specs=pl.BlockSpec((None, T, gw), row),
        out_shape=jax.ShapeDtypeStruct((B, T, DQ), BF16),
        scratch_shapes=[
            pltpu.VMEM((HEAD_GROUP, 2 * T, hd), BF16),
            pltpu.VMEM((HEAD_GROUP * 2 * T, 1), F32),
            pltpu.VMEM((HEAD_GROUP * 2 * T, 1), F32),
            pltpu.VMEM((HEAD_GROUP * 2 * T, hd), F32),
        ],
        compiler_params=_params("parallel", "parallel", "arbitrary"),
        name="diff_attn_sample",
    )(q, k5, k5, v_past, v_past, k_new, v_new, *lams, g_head.reshape(1, hd))


def _rope_tables(pos):
    half = ROT_DIM // 2
    inv = ROPE_THETA ** (-jnp.arange(0, ROT_DIM, 2, dtype=F32) / ROT_DIM)
    ang = pos.astype(F32)[:, None] * inv[None, :]
    cos, sin = jnp.cos(ang), jnp.sin(ang)
    T = pos.shape[0]
    ones = jnp.ones((T, DIFF_HEAD_DIM - ROT_DIM), F32)
    zeros_h = jnp.zeros((T, half), F32)
    zeros_r = jnp.zeros((T, DIFF_HEAD_DIM - ROT_DIM), F32)
    c = jnp.concatenate([cos, cos, ones], axis=1)
    sa = jnp.concatenate([-sin, zeros_h, zeros_r], axis=1)
    sb = jnp.concatenate([zeros_h, sin, zeros_r], axis=1)
    return c, sa, sb


def _trunk(x, mods, pos, s0, k_past, v_past, W):
    (g_norm, w_ffn_up, w_ffn_down, w_gla_in_t, w_gla_gate, b_gla_gate, g_gla_head, w_gla_out,
     g_kv, w_kv, w_dq, lams, g_diff_head, w_do, g_final) = W
    B, T, D = x.shape
    depth = g_norm.shape[0]
    n_a = w_gla_in_t.shape[0]
    tables = _rope_tables(pos)
    dk_all = w_gla_gate.shape[-1]
    dv_all = w_gla_out.shape[-2]
    d_qk = w_dq.shape[-1]

    h = x
    states = []
    k_sh = v_sh = None
    for l in range(depth):
        mod = mods[l]

        def ffn(h, s, i0):
            hn = _norm_call(h, g_norm[l, i0 // 3], mod, i0, i0 + 1)
            hid = _mm_swiglu(hn, w_ffn_up, (l, s))
            return _mm_down(hid, w_ffn_down, (l, s), h, mod, i0 + 2, 0.5)

        h = ffn(h, 0, 0)
        hn = _norm_call(h, g_norm[l, 1], mod, 3, 4)
        if l < n_a:
            n_qkvr = 2 * dk_all + 2 * dv_all
            p = _mm_plain_t(hn, w_gla_in_t, (l,), n_qkvr, "gla_in")
            la = _gla_gate_call(hn, w_gla_in_t[l, n_qkvr:], w_gla_gate[l], b_gla_gate[l])
            og, S = _gla_call(p, la, g_gla_head[l], s0, l, dk_all, dv_all)
            states.append(S)
            h = _mm_resid(og, w_gla_out, (l,), h, mod, 5, 1.0, "gla_out")
        else:
            jj = l - n_a
            lam_init = 0.8 - 0.6 * math.exp(-0.3 * l)
            q = _mm_rope(hn, w_dq, (jj,), d_qk, tables, "diff_q")
            lam_l = tuple(t[jj].reshape(1, DIFF_HEAD_DIM) for t in lams)
            if k_past is None:
                oa = _attn_prompt_call(q, k_sh, v_sh, lam_l, g_diff_head[jj], lam_init)
            else:
                oa = _attn_sample_call(q, k_past, v_past, k_sh, v_sh, lam_l, g_diff_head[jj], lam_init)
            h = _mm_resid(oa, w_do, (jj,), h, mod, 5, 1.0, "diff_out")
        h = ffn(h, 1, 6)
        if l == n_a - 1:
            xn = _norm_call(h, g_kv)
            k_sh = _mm_rope(xn, w_kv, (), d_qk, tables, "shared_k")
            v_sh = _mm_plain(xn, w_kv, (), d_qk, w_kv.shape[-1] - d_qk, "shared_v")
    y = _norm_call(h, g_final, out_dtype=F32)
    state = states[0][None] if len(states) == 1 else jnp.stack(states)
    return y, state, k_sh, v_sh


def kernel(x_prompt, x_sample, c_prompt, c_sample, state_gla, cache_k, cache_v, w_mod, b_mod, g_norm, w_ffn_up, w_ffn_down, w_gla_in, w_gla_gate, b_gla_gate, g_gla_head, w_gla_out, g_kv, w_kv, w_dq, lambda_q1, lambda_k1, lambda_q2, lambda_k2, g_diff_head, w_do, g_final):
    B, T, D = x_prompt.shape
    Bs, Ts, _ = x_sample.shape
    P = cache_k.shape[1]
    H = cache_k.shape[2]
    hd = 2 * DIFF_HEAD_DIM

    rows = B + Bs
    rows_pad = -(-rows // 16) * 16
    c_all = jnp.concatenate([c_prompt, c_sample, jnp.zeros((rows_pad - rows, D), F32)], axis=0)
    mod_all = _mod_call(c_all, w_mod, b_mod)
    L = w_mod.shape[0]
    mods_p = [mod_all[l, :B].reshape(B, N_MOD, 1, D) for l in range(L)]
    mods_s = [mod_all[l, B:rows].reshape(Bs, N_MOD, 1, D) for l in range(L)]

    w_gla_in_t = jnp.swapaxes(w_gla_in, 1, 2)
    W = (g_norm, w_ffn_up, w_ffn_down, w_gla_in_t, w_gla_gate, b_gla_gate, g_gla_head, w_gla_out,
         g_kv, w_kv, w_dq, (lambda_q1, lambda_k1, lambda_q2, lambda_k2), g_diff_head, w_do, g_final)

    y_p, s_p, k_p, v_p = _trunk(x_prompt, mods_p, jnp.arange(T), None, None, None, W)
    y_s, s_s, k_s, v_s = _trunk(x_sample, mods_s, P + jnp.arange(Ts), state_gla, cache_k, cache_v, W)
    return (y_p, y_s, s_p, s_s,
            k_p.reshape(B, T, H, 2, DIFF_HEAD_DIM), v_p.reshape(B, T, H, hd),
            k_s.reshape(Bs, Ts, H, 2, DIFF_HEAD_DIM), v_s.reshape(Bs, Ts, H, hd))
```
